```python
import math
import jax, jax.numpy as jnp
from jax import lax
import numpy as np

D_MODEL = 1024
BATCH = 1
SEQ = 16384
DEPTH = 1

MEM_LEN = 256
HEAD_DIM = 64
MIX_WIDTH = D_MODEL
DA_WIDTH = MIX_WIDTH // 2
SB_WIDTH = MIX_WIDTH - DA_WIDTH
DA_HEADS = DA_WIDTH // (2 * HEAD_DIM)
DA_V = 2 * HEAD_DIM
SB_HEADS = SB_WIDTH // HEAD_DIM
DA_Q_COLS = DA_HEADS * 2 * HEAD_DIM
DA_K_COLS = DA_HEADS * 2 * HEAD_DIM
DA_V_COLS = DA_HEADS * DA_V
SB_COLS = SB_HEADS * HEAD_DIM
IN_COLS = DA_Q_COLS + DA_K_COLS + DA_V_COLS + 3 * SB_COLS
ROT_DIM = HEAD_DIM // 4
ROPE_THETA = 500000.0
X_HEADS = 4
X_HEAD_DIM = D_MODEL // X_HEADS
D_FF = 4 * D_MODEL
Q_BLOCK = 128
EPS = 1e-6

kernel_name = "hybrid_diffattn_stickbreaking_block"


def rms_norm(x, g):
    xf = x.astype(jnp.float32)
    y = xf * lax.rsqrt(jnp.mean(xf * xf, axis=-1, keepdims=True) + EPS)
    return (y * g.astype(jnp.float32)).astype(x.dtype)


def rope_tables(positions, dtype):
    inv_freq = ROPE_THETA ** (-jnp.arange(0, ROT_DIM, 2, dtype=jnp.float32) / ROT_DIM)
    ang = positions.astype(jnp.float32)[..., None] * inv_freq
    return jnp.cos(ang).astype(dtype), jnp.sin(ang).astype(dtype)


def apply_partial_rope(x, cos, sin):
    extra = x.ndim - 3
    c = cos.reshape(cos.shape[:2] + (1,) * extra + cos.shape[-1:])
    s = sin.reshape(sin.shape[:2] + (1,) * extra + sin.shape[-1:])
    half = ROT_DIM // 2
    x1, x2, xp = x[..., :half], x[..., half:ROT_DIM], x[..., ROT_DIM:]
    return jnp.concatenate([x1 * c - x2 * s, x2 * c + x1 * s, xp], axis=-1)


def diff_attention(q, k, v, lam, g_subln, lam_init):
    B, S, H = q.shape[0], q.shape[1], q.shape[2]
    nb = S // Q_BLOCK
    scale = 1.0 / math.sqrt(HEAD_DIM)
    kh = k.transpose(0, 2, 3, 1, 4)
    vh = v.transpose(0, 2, 1, 3).astype(jnp.float32)
    qb = q.transpose(0, 2, 3, 1, 4).reshape(B, H, 2, nb, Q_BLOCK, HEAD_DIM)
    qb = jnp.moveaxis(qb, 3, 0)
    qidx = jnp.arange(S, dtype=jnp.int32).reshape(nb, Q_BLOCK)
    kidx = jnp.arange(S, dtype=jnp.int32)

    def block(args):
        qblk, qi = args
        s = jnp.einsum('bhcqd,bhckd->bhcqk', qblk, kh).astype(jnp.float32) * scale
        mask = kidx[None, :] <= qi[:, None]
        p = jax.nn.softmax(jnp.where(mask, s, -jnp.inf), axis=-1)
        w = p[:, :, 0] - lam * p[:, :, 1]
        return jnp.einsum('bhqk,bhkv->bhqv', w, vh)

    o = lax.map(block, (qb, qidx))
    o = o.transpose(1, 0, 3, 2, 4).reshape(B, S, H, DA_V)
    o = rms_norm(o, g_subln) * (1.0 - lam_init)
    return o.reshape(B, S, H * DA_V)


def stick_breaking_attention(q, k, v):
    B, S, H = q.shape[0], q.shape[1], q.shape[2]
    nb = S // Q_BLOCK
    scale = 1.0 / math.sqrt(HEAD_DIM)
    kh = k.transpose(0, 2, 1, 3)
    vh = v.transpose(0, 2, 1, 3).astype(jnp.float32)
    qb = q.transpose(0, 2, 1, 3).reshape(B, H, nb, Q_BLOCK, HEAD_DIM)
    qb = jnp.moveaxis(qb, 2, 0)
    qidx = jnp.arange(S, dtype=jnp.int32).reshape(nb, Q_BLOCK)
    kidx = jnp.arange(S, dtype=jnp.int32)

    def block(args):
        qblk, qi = args
        z = jnp.einsum('bhqd,bhkd->bhqk', qblk, kh).astype(jnp.float32) * scale
        strict = kidx[None, :] < qi[:, None]
        log_beta = jax.nn.log_sigmoid(z)
        log_1mb = jnp.where(strict, jax.nn.log_sigmoid(-z), 0.0)
        between = lax.cumsum(log_1mb, axis=3, reverse=True) - log_1mb
        a = jnp.exp(jnp.where(strict, log_beta + between, -jnp.inf))
        return jnp.einsum('bhqk,bhkd->bhqd', a, vh)

    o = lax.map(block, (qb, qidx))
    return o.transpose(1, 0, 3, 2, 4).reshape(B, S, H * HEAD_DIM)


def cross_attention(h, memn, w_xq, w_xkv, w_xo):
    B, S, _ = h.shape
    q = (h @ w_xq).reshape(B, S, X_HEADS, X_HEAD_DIM)
    kv = memn @ w_xkv
    k = kv[..., :D_MODEL].reshape(B, MEM_LEN, X_HEADS, X_HEAD_DIM)
    v = kv[..., D_MODEL:].reshape(B, MEM_LEN, X_HEADS, X_HEAD_DIM).astype(jnp.float32)
    s = jnp.einsum('bqhd,bkhd->bhqk', q, k).astype(jnp.float32) / math.sqrt(X_HEAD_DIM)
    p = jax.nn.softmax(s, axis=-1)
    o = jnp.einsum('bhqk,bkhd->bqhd', p, v).reshape(B, S, D_MODEL).astype(h.dtype)
    return o @ w_xo


def setup_inputs(seed: int = 0) -> dict:
    key = jax.random.key(seed)
    ks = jax.random.split(key, 24)
    f32 = jnp.float32

    def w(k, shape, fan_in):
        return jax.random.normal(k, shape, f32) * (fan_in ** -0.5)

    def gain(k, shape):
        return 1.0 + 0.02 * jax.random.normal(k, shape, f32)

    L = DEPTH
    return {
        "x": jax.random.normal(ks[0], (BATCH, SEQ, D_MODEL), f32),
        "mem": jax.random.normal(ks[1], (BATCH, MEM_LEN, D_MODEL), f32),
        "positions": jnp.broadcast_to(jnp.arange(SEQ, dtype=jnp.int32), (BATCH, SEQ)),
        "g_mix": gain(ks[2], (L, D_MODEL)),
        "w_in": w(ks[3], (L, D_MODEL, IN_COLS), D_MODEL),
        "lambda_q1": 0.1 * jax.random.normal(ks[4], (L, HEAD_DIM), f32),
        "lambda_k1": 0.1 * jax.random.normal(ks[5], (L, HEAD_DIM), f32),
        "lambda_q2": 0.1 * jax.random.normal(ks[6], (L, HEAD_DIM), f32),
        "lambda_k2": 0.1 * jax.random.normal(ks[7], (L, HEAD_DIM), f32),
        "g_subln": gain(ks[8], (L, DA_V)),
        "w_out": w(ks[9], (L, MIX_WIDTH, D_MODEL), MIX_WIDTH),
        "g_cross": gain(ks[10], (L, D_MODEL)),
        "g_mem": gain(ks[11], (L, D_MODEL)),
        "w_xq": w(ks[12], (L, D_MODEL, D_MODEL), D_MODEL),
        "w_xkv": w(ks[13], (L, D_MODEL, 2 * D_MODEL), D_MODEL),
        "w_xo": w(ks[14], (L, D_MODEL, D_MODEL), D_MODEL),
        "g_mlp": gain(ks[15], (L, D_MODEL)),
        "w_up": w(ks[16], (L, D_MODEL, D_FF), D_MODEL),
        "w_down": w(ks[17], (L, D_FF, D_MODEL), D_FF),
        "g_final": gain(ks[18], (D_MODEL,)),
    }


def reference(x, mem, positions, g_mix, w_in, lambda_q1, lambda_k1, lambda_q2, lambda_k2,
              g_subln, w_out, g_cross, g_mem, w_xq, w_xkv, w_xo, g_mlp, w_up, w_down,
              g_final):
    B, S, _ = x.shape
    cos, sin = rope_tables(positions, x.dtype)
    h = x
    for l in range(DEPTH):
        lam_init = 0.8 - 0.6 * math.exp(-0.3 * l)
        u = rms_norm(h, g_mix[l]) @ w_in[l]
        o0 = 0
        qa = u[..., o0:o0 + DA_Q_COLS].reshape(B, S, DA_HEADS, 2, HEAD_DIM); o0 += DA_Q_COLS
        ka = u[..., o0:o0 + DA_K_COLS].reshape(B, S, DA_HEADS, 2, HEAD_DIM); o0 += DA_K_COLS
        va = u[..., o0:o0 + DA_V_COLS].reshape(B, S, DA_HEADS, DA_V); o0 += DA_V_COLS
        qs = u[..., o0:o0 + SB_COLS].reshape(B, S, SB_HEADS, HEAD_DIM); o0 += SB_COLS
        ksb = u[..., o0:o0 + SB_COLS].reshape(B, S, SB_HEADS, HEAD_DIM); o0 += SB_COLS
        vs = u[..., o0:o0 + SB_COLS].reshape(B, S, SB_HEADS, HEAD_DIM)
        qa = apply_partial_rope(qa, cos, sin)
        ka = apply_partial_rope(ka, cos, sin)
        lam = (jnp.exp(jnp.sum(lambda_q1[l].astype(jnp.float32) * lambda_k1[l].astype(jnp.float32)))
               - jnp.exp(jnp.sum(lambda_q2[l].astype(jnp.float32) * lambda_k2[l].astype(jnp.float32)))
               + lam_init)
        ya = diff_attention(qa, ka, va, lam, g_subln[l], lam_init)
        yb = stick_breaking_attention(qs, ksb, vs)
        y = jnp.concatenate([ya.astype(h.dtype), yb.astype(h.dtype)], axis=-1)
        h = h + y @ w_out[l]
        h = h + cross_attention(rms_norm(h, g_cross[l]), rms_norm(mem, g_mem[l]),
                                w_xq[l], w_xkv[l], w_xo[l])
        a = jnp.square(jax.nn.relu(rms_norm(h, g_mlp[l]) @ w_up[l]))
        h = h + a @ w_down[l]
    return rms_norm(h, g_final)
```

```python
import functools
import math

import numpy as np
import jax
import jax.numpy as jnp
from jax import lax
from jax.experimental import pallas as pl
from jax.experimental.pallas import tpu as pltpu

F32 = jnp.float32
BF16 = jnp.bfloat16

HEAD_DIM = 64
DA_HEADS = 4
DA_V = 2 * HEAD_DIM
SB_HEADS = 8
GROUP_COLS = 512
ROT_DIM = HEAD_DIM // 4
ROPE_THETA = 500000.0
X_HEADS = 4
EPS = 1e-6
LANES = 128
NEG_BIG = -1e30

SB_EXP_ZERO = -104.0

VMEM_LIMIT = 56 * 1024 * 1024

ROW_TILE = 512
ATT_TILE = 256
V_ROWS = DA_V + 16


def _rms(x, g):
    ms = jnp.mean(x * x, axis=-1, keepdims=True)
    return x * lax.rsqrt(ms + EPS) * g


def _const_spec(shape):
    nd = len(shape)
    return pl.BlockSpec(shape, lambda *_: (0,) * nd, pipeline_mode=pl.Buffered(1))


def _in_proj_kernel(x_ref, pos_ref, invf_ref, g_ref, w_ref,
                    qa_ref, ka_ref, va_ref, qs_ref, ks_ref, vs_ref):
    xn = _rms(x_ref[...], g_ref[...]).astype(BF16)

    ang = pos_ref[...].astype(F32) * invf_ref[...]
    cos = jnp.cos(ang)
    sin = jnp.sin(ang)
    j = lax.broadcasted_iota(jnp.int32, (1, LANES), 1) % HEAD_DIM
    s_up = jnp.where(j < ROT_DIM // 2, -sin, 0.0)
    s_dn = jnp.where(j >= ROT_DIM // 2, sin, 0.0)

    def group(idx):
        w = w_ref[:, idx * GROUP_COLS:(idx + 1) * GROUP_COLS]
        return jnp.dot(xn, w, preferred_element_type=F32)

    def rope(u, scale):
        parts = []
        for c in range(GROUP_COLS // LANES):
            uc = u[:, c * LANES:(c + 1) * LANES]
            r = (uc * cos + pltpu.roll(uc, LANES - ROT_DIM // 2, axis=1) * s_up
                 + pltpu.roll(uc, ROT_DIM // 2, axis=1) * s_dn)
            parts.append(r * scale)
        return jnp.concatenate(parts, axis=1)

    inv_sqrt_d = 1.0 / math.sqrt(HEAD_DIM)
    qa_ref[...] = rope(group(0), inv_sqrt_d).astype(BF16)
    ka_ref[...] = rope(group(1), 1.0).astype(BF16)
    va_ref[...] = group(2).astype(BF16)
    qs_ref[...] = (group(3) * inv_sqrt_d).astype(BF16)
    ks_ref[...] = group(4).astype(BF16)
    vs_ref[...] = group(5).astype(BF16)


def _in_proj(x, pos_col, invf, g, w):
    S, D = x.shape
    ts = min(ROW_TILE, S)
    out = jax.ShapeDtypeStruct((S, GROUP_COLS), BF16)
    row = lambda i: (i, 0)
    return pl.pallas_call(
        _in_proj_kernel,
        grid=(S // ts,),
        in_specs=[
            pl.BlockSpec((ts, D), row),
            pl.BlockSpec((ts, 1), row),
            _const_spec((1, LANES)),
            _const_spec((1, D)),
            _const_spec(w.shape),
        ],
        out_specs=[pl.BlockSpec((ts, GROUP_COLS), row)] * 6,
        out_shape=[out] * 6,
        compiler_params=pltpu.CompilerParams(
            dimension_semantics=("arbitrary",), vmem_limit_bytes=VMEM_LIMIT),
        name="in_proj",
    )(x, pos_col, invf, g, w)


def _diff_attn_kernel(lam_ref, g_ref, qt_ref, k_ref, vt_ref, o_ref, acc_ref, m_ref,
                      *, lam_init):
    i = pl.program_id(1)
    tq = qt_ref.shape[-1]
    acc_ref[...] = jnp.zeros_like(acc_ref)
    m_ref[...] = jnp.full_like(m_ref, NEG_BIG)

    def step(j, diagonal):
        kb = k_ref[j]
        vb = vt_ref[j]
        for c in range(2):
            st = jnp.dot(kb, qt_ref[c], preferred_element_type=F32)
            if diagonal:
                key = lax.broadcasted_iota(jnp.int32, st.shape, 0)
                qry = lax.broadcasted_iota(jnp.int32, st.shape, 1)
                st = jnp.where(key <= qry, st, NEG_BIG)
            m_old = m_ref[c]
            m_new = jnp.maximum(m_old, jnp.max(st, axis=0, keepdims=True))
            alpha = jnp.exp(m_old - m_new)
            p = jnp.exp(st - m_new).astype(BF16)
            acc_ref[c] = alpha * acc_ref[c] + jnp.dot(vb, p, preferred_element_type=F32)
            m_ref[c] = m_new

    def body(j, carry):
        step(j, False)
        return carry

    lax.fori_loop(0, i, body, 0)
    step(i, True)

    lam_v = lam_ref[...]
    lam = (jnp.exp(jnp.sum(lam_v[0:1] * lam_v[1:2], axis=1, keepdims=True))
           - jnp.exp(jnp.sum(lam_v[2:3] * lam_v[3:4], axis=1, keepdims=True)) + lam_init)
    a0 = acc_ref[0]
    a1 = acc_ref[1]
    o = a0[:DA_V] / a0[DA_V:DA_V + 1] - lam * (a1[:DA_V] / a1[DA_V:DA_V + 1])
    ms = jnp.mean(o * o, axis=0, keepdims=True)
    y = o * lax.rsqrt(ms + EPS) * g_ref[...] * (1.0 - lam_init)
    o_ref[...] = y.T.astype(o_ref.dtype)


def _diff_attn(lam_vecs, g_col, qt, kblk, vtblk, lam_init):
    H, _, _, S = qt.shape
    nk, tk = kblk.shape[1], kblk.shape[2]
    tq = tk
    return pl.pallas_call(
        functools.partial(_diff_attn_kernel, lam_init=lam_init),
        grid=(H, S // tq),
        in_specs=[
            _const_spec(lam_vecs.shape),
            _const_spec(g_col.shape),
            pl.BlockSpec((None, 2, LANES, tq), lambda h, i: (h, 0, 0, i)),
            pl.BlockSpec((None, nk, tk, LANES), lambda h, i: (h, 0, 0, 0)),
            pl.BlockSpec((None, nk, V_ROWS, tk), lambda h, i: (h, 0, 0, 0)),
        ],
        out_specs=pl.BlockSpec((tq, DA_V), lambda h, i: (i, h)),
        out_shape=jax.ShapeDtypeStruct((S, H * DA_V), BF16),
        scratch_shapes=[pltpu.VMEM((2, V_ROWS, tq), F32), pltpu.VMEM((2, 1, tq), F32)],
        compiler_params=pltpu.CompilerParams(
            dimension_semantics=("arbitrary", "arbitrary"), vmem_limit_bytes=VMEM_LIMIT),
        name="diff_attn",
    )(lam_vecs, g_col, qt, kblk, vtblk)


def _sb_attn_kernel(q_ref, k_ref, v_ref, o_ref):
    i = pl.program_id(1)
    tq = q_ref.shape[0]
    tk = k_ref.shape[1]
    q2 = q_ref[...]
    lane = lax.broadcasted_iota(jnp.int32, q2.shape, 1)
    diff = (lax.broadcasted_iota(jnp.int32, (tq, tk), 1)
            - lax.broadcasted_iota(jnp.int32, (tq, tk), 0))
    later = (lax.broadcasted_iota(jnp.int32, (tk, tk), 0)
             > lax.broadcasted_iota(jnp.int32, (tk, tk), 1))
    tri = jnp.where(later, 1.0, 0.0).astype(BF16)

    def one_head(half):
        mine = (lane >= half * HEAD_DIM) & (lane < (half + 1) * HEAD_DIM)
        qh = jnp.where(mine, q2, jnp.zeros_like(q2))

        def cond(state):
            j, _, _, top = state
            return jnp.logical_and(j >= 0, top > SB_EXP_ZERO)

        def body(state):
            j, carry, acc, _ = state
            kb = k_ref[j]
            vb = v_ref[j]
            z = lax.dot_general(qh, kb, (((1,), (1,)), ((), ())),
                                preferred_element_type=F32)
            strict = diff < (i - j) * tq
            t = jnp.log1p(jnp.exp(-jnp.abs(z)))
            log_beta = -(jnp.maximum(-z, 0.0) + t)
            log_1mb = jnp.where(strict, -(jnp.maximum(z, 0.0) + t), 0.0)
            hi = log_1mb.astype(BF16)
            lo = (log_1mb - hi.astype(F32)).astype(BF16)
            between = (jnp.dot(hi, tri, preferred_element_type=F32)
                       + jnp.dot(lo, tri, preferred_element_type=F32) + carry)
            a = jnp.where(strict, jnp.exp(log_beta + between), 0.0).astype(BF16)
            acc = acc + jnp.dot(a, vb, preferred_element_type=F32)
            carry = carry + jnp.sum(log_1mb, axis=1, keepdims=True)
            return j - 1, carry, acc, jnp.max(carry)

        init = (i, jnp.zeros((tq, 1), F32), jnp.zeros((tq, LANES), F32), jnp.float32(0.0))
        return lax.while_loop(cond, body, init)[2]

    o = jnp.where(lane < HEAD_DIM, one_head(0), one_head(1))
    o_ref[...] = o.astype(o_ref.dtype)


def _sb_attn(q, kblk, vblk):
    S = q.shape[0]
    P, nk, tk, _ = kblk.shape
    tq = tk
    return pl.pallas_call(
        _sb_attn_kernel,
        grid=(P, S // tq),
        in_specs=[
            pl.BlockSpec((tq, LANES), lambda p, i: (i, p)),
            pl.BlockSpec((None, nk, tk, LANES), lambda p, i: (p, 0, 0, 0)),
            pl.BlockSpec((None, nk, tk, LANES), lambda p, i: (p, 0, 0, 0)),
        ],
        out_specs=pl.BlockSpec((tq, LANES), lambda p, i: (i, p)),
        out_shape=jax.ShapeDtypeStruct((S, P * LANES), BF16),
        compiler_params=pltpu.CompilerParams(
            dimension_semantics=("arbitrary", "arbitrary"), vmem_limit_bytes=VMEM_LIMIT),
        name="sb_attn",
    )(q, kblk, vblk)


def _mem_kv_kernel(mem_ref, g_ref, w_ref, kv_ref):
    mn = _rms(mem_ref[...], g_ref[...]).astype(BF16)
    kv_ref[...] = jnp.dot(mn, w_ref[...], preferred_element_type=F32).astype(kv_ref.dtype)


def _mem_kv(mem, g, w):
    M = mem.shape[0]
    return pl.pallas_call(
        _mem_kv_kernel,
        out_shape=jax.ShapeDtypeStruct((M, w.shape[1]), BF16),
        compiler_params=pltpu.CompilerParams(vmem_limit_bytes=VMEM_LIMIT),
        name="mem_kv",
    )(mem, g, w)


def _post_attn_kernel(x_ref, ya_ref, yb_ref, wout_ref, g_ref, wxq_ref, kv_ref, wxo_ref, h_ref):
    D = x_ref.shape[1]
    half = ya_ref.shape[1]
    h = (x_ref[...]
         + jnp.dot(ya_ref[...], wout_ref[:half], preferred_element_type=F32)
         + jnp.dot(yb_ref[...], wout_ref[half:], preferred_element_type=F32))
    hn = _rms(h, g_ref[...]).astype(BF16)
    xd = D // X_HEADS
    q = jnp.dot(hn, wxq_ref[...], preferred_element_type=F32) * (1.0 / math.sqrt(xd))
    q = q.astype(BF16)
    upd = jnp.zeros_like(h)
    for hd in range(X_HEADS):
        qh = q[:, hd * xd:(hd + 1) * xd]
        kh = kv_ref[:, hd * xd:(hd + 1) * xd]
        vh = kv_ref[:, D + hd * xd:D + (hd + 1) * xd]
        s = lax.dot_general(qh, kh, (((1,), (1,)), ((), ())), preferred_element_type=F32)
        p = jnp.exp(s - jnp.max(s, axis=-1, keepdims=True))
        l = jnp.sum(p, axis=-1, keepdims=True)
        oh = jnp.dot(p.astype(BF16), vh, preferred_element_type=F32) / l
        upd = upd + jnp.dot(oh.astype(BF16), wxo_ref[hd * xd:(hd + 1) * xd, :],
                            preferred_element_type=F32)
    h_ref[...] = h + upd


def _post_attn(x, ya, yb, wout, g, wxq, kv, wxo):
    S, D = x.shape
    ts = min(ROW_TILE, S)
    row = lambda i: (i, 0)
    return pl.pallas_call(
        _post_attn_kernel,
        grid=(S // ts,),
        in_specs=[
            pl.BlockSpec((ts, D), row),
            pl.BlockSpec((ts, ya.shape[1]), row),
            pl.BlockSpec((ts, yb.shape[1]), row),
            _const_spec(wout.shape),
            _const_spec(g.shape),
            _const_spec(wxq.shape),
            _const_spec(kv.shape),
            _const_spec(wxo.shape),
        ],
        out_specs=pl.BlockSpec((ts, D), row),
        out_shape=jax.ShapeDtypeStruct((S, D), F32),
        compiler_params=pltpu.CompilerParams(
            dimension_semantics=("arbitrary",), vmem_limit_bytes=VMEM_LIMIT),
        name="post_attn",
    )(x, ya, yb, wout, g, wxq, kv, wxo)


def _mlp_kernel(h_ref, g_ref, wup_ref, wdown_ref, gf_ref, o_ref, *, final_norm):
    h = h_ref[...]
    D = h.shape[1]
    hn = _rms(h, g_ref[...]).astype(BF16)
    upd = jnp.zeros_like(h)
    for c in range(wup_ref.shape[1] // D):
        a = jnp.dot(hn, wup_ref[:, c * D:(c + 1) * D], preferred_element_type=F32)
        a = jnp.square(jnp.maximum(a, 0.0)).astype(BF16)
        upd = upd + jnp.dot(a, wdown_ref[c * D:(c + 1) * D, :], preferred_element_type=F32)
    h = h + upd
    if final_norm:
        h = _rms(h, gf_ref[...])
    o_ref[...] = h


def _mlp(h, g, wup, wdown, gf, final_norm):
    S, D = h.shape
    ts = min(ROW_TILE, S)
    row = lambda i: (i, 0)
    return pl.pallas_call(
        functools.partial(_mlp_kernel, final_norm=final_norm),
        grid=(S // ts,),
        in_specs=[
            pl.BlockSpec((ts, D), row),
            _const_spec(g.shape),
            _const_spec(wup.shape),
            _const_spec(wdown.shape),
            _const_spec(gf.shape),
        ],
        out_specs=pl.BlockSpec((ts, D), row),
        out_shape=jax.ShapeDtypeStruct((S, D), F32),
        compiler_params=pltpu.CompilerParams(
            dimension_semantics=("arbitrary",), vmem_limit_bytes=VMEM_LIMIT),
        name="mlp",
    )(h, g, wup, wdown, gf)


def _rope_inv_freq_lanes():
    inv = np.float32(ROPE_THETA) ** (-np.arange(0, ROT_DIM, 2, dtype=np.float32) / np.float32(ROT_DIM))
    lanes = np.zeros((1, LANES), np.float32)
    for head in range(LANES // HEAD_DIM):
        base = head * HEAD_DIM
        lanes[0, base:base + ROT_DIM // 2] = inv
        lanes[0, base + ROT_DIM // 2:base + ROT_DIM] = inv
    return jnp.asarray(lanes)


def _layer(h, pos_col, invf, mem, p, lam_init, gf, final_norm):
    S, D = h.shape
    t = min(ATT_TILE, S)
    nk = S // t
    qa, ka, va, qs, ks, vs = _in_proj(h, pos_col, invf, p["g_mix"], p["w_in"])

    q4 = qa.reshape(S, DA_HEADS, 2, HEAD_DIM).transpose(1, 2, 3, 0)
    zeros = jnp.zeros_like(q4[:, 0])
    qt = jnp.stack([jnp.concatenate([q4[:, 0], zeros], axis=1),
                    jnp.concatenate([zeros, q4[:, 1]], axis=1)], axis=1)
    kblk = ka.reshape(nk, t, DA_HEADS, LANES).transpose(2, 0, 1, 3)
    vt = va.reshape(nk, t, DA_HEADS, DA_V).transpose(2, 0, 3, 1)
    ones_pad = jnp.zeros((DA_HEADS, nk, V_ROWS - DA_V, t), BF16).at[:, :, 0, :].set(1.0)
    vtblk = jnp.concatenate([vt, ones_pad], axis=2)
    ya = _diff_attn(p["lam_vecs"], p["g_subln"], qt, kblk, vtblk, lam_init)

    pairs = SB_HEADS // 2
    ksb = ks.reshape(nk, t, pairs, LANES).transpose(2, 0, 1, 3)
    vsb = vs.reshape(nk, t, pairs, LANES).transpose(2, 0, 1, 3)
    yb = _sb_attn(qs, ksb, vsb)

    kv = _mem_kv(mem, p["g_mem"], p["w_xkv"])
    h = _post_attn(h, ya, yb, p["w_out"], p["g_cross"], p["w_xq"], kv, p["w_xo"])
    return _mlp(h, p["g_mlp"], p["w_up"], p["w_down"], gf, final_norm)


def kernel(x, mem, positions, g_mix, w_in, lambda_q1, lambda_k1, lambda_q2, lambda_k2, g_subln, w_out, g_cross, g_mem, w_xq, w_xkv, w_xo, g_mlp, w_up, w_down, g_final):
    B, S, D = x.shape
    depth = w_in.shape[0]
    invf = _rope_inv_freq_lanes()
    gf = g_final.reshape(1, D).astype(F32)
    outs = []
    for b in range(B):
        h = x[b]
        pos_col = positions[b].reshape(S, 1)
        for l in range(depth):
            lam_init = 0.8 - 0.6 * math.exp(-0.3 * l)
            p = {
                "g_mix": g_mix[l].reshape(1, D),
                "w_in": w_in[l].astype(BF16),
                "lam_vecs": jnp.stack([lambda_q1[l], lambda_k1[l], lambda_q2[l], lambda_k2[l]]).astype(F32),
                "g_subln": g_subln[l].reshape(DA_V, 1).astype(F32),
                "w_out": w_out[l].astype(BF16),
                "g_cross": g_cross[l].reshape(1, D),
                "g_mem": g_mem[l].reshape(1, D),
                "w_xq": w_xq[l].astype(BF16),
                "w_xkv": w_xkv[l].astype(BF16),
                "w_xo": w_xo[l].astype(BF16),
                "g_mlp": g_mlp[l].reshape(1, D),
                "w_up": w_up[l].astype(BF16),
                "w_down": w_down[l].astype(BF16),
            }
            h = _layer(h, pos_col, invf, mem[b], p, lam_init, gf, l == depth - 1)
        outs.append(h)
    return jnp.stack(outs)
```

```python
import functools
import math

import numpy as np
import jax
import jax.numpy as jnp
from jax import lax
from jax.experimental import pallas as pl
from jax.experimental.pallas import tpu as pltpu

F32 = jnp.float32
BF16 = jnp.bfloat16

HEAD_DIM = 64
DA_HEADS = 4
DA_V = 2 * HEAD_DIM
SB_HEADS = 8
GROUP_COLS = 512
ROT_DIM = HEAD_DIM // 4
ROPE_THETA = 500000.0
X_HEADS = 4
EPS = 1e-6
LANES = 128
NEG_BIG = -1e30

SB_EXP_ZERO = -104.0

VMEM_LIMIT = 56 * 1024 * 1024

ROW_TILE = 512
DIFF_TILE = 512
SB_TILE = 256
V_ROWS = DA_V + 16


def _rms(x, g):
    ms = jnp.mean(x * x, axis=-1, keepdims=True)
    return x * lax.rsqrt(ms + EPS) * g


def _const_spec(shape):
    nd = len(shape)
    return pl.BlockSpec(shape, lambda *_: (0,) * nd, pipeline_mode=pl.Buffered(1))


def _in_proj_kernel(x_ref, pos_ref, invf_ref, g_ref, w_ref,
                    qa_ref, ka_ref, va_ref, qs_ref, ks_ref, vs_ref):
    xn = _rms(x_ref[...], g_ref[...]).astype(BF16)

    ang = pos_ref[...].astype(F32) * invf_ref[...]
    cos = jnp.cos(ang)
    sin = jnp.sin(ang)
    j = lax.broadcasted_iota(jnp.int32, (1, LANES), 1) % HEAD_DIM
    s_up = jnp.where(j < ROT_DIM // 2, -sin, 0.0)
    s_dn = jnp.where(j >= ROT_DIM // 2, sin, 0.0)

    def group(idx):
        w = w_ref[:, idx * GROUP_COLS:(idx + 1) * GROUP_COLS]
        return jnp.dot(xn, w, preferred_element_type=F32)

    def rope(u, scale):
        parts = []
        for c in range(GROUP_COLS // LANES):
            uc = u[:, c * LANES:(c + 1) * LANES]
            r = (uc * cos + pltpu.roll(uc, LANES - ROT_DIM // 2, axis=1) * s_up
                 + pltpu.roll(uc, ROT_DIM // 2, axis=1) * s_dn)
            parts.append(r * scale)
        return jnp.concatenate(parts, axis=1)

    inv_sqrt_d = 1.0 / math.sqrt(HEAD_DIM)
    qa_ref[...] = rope(group(0), inv_sqrt_d * math.log2(math.e)).astype(BF16)
    ka_ref[...] = rope(group(1), 1.0).astype(BF16)
    va_ref[...] = group(2).astype(BF16)
    qs_ref[...] = (group(3) * inv_sqrt_d).astype(BF16)
    ks_ref[...] = group(4).astype(BF16)
    vs_ref[...] = group(5).astype(BF16)


def _in_proj(x, pos_col, invf, g, w):
    S, D = x.shape
    ts = min(ROW_TILE, S)
    out = jax.ShapeDtypeStruct((S, GROUP_COLS), BF16)
    row = lambda i: (i, 0)
    return pl.pallas_call(
        _in_proj_kernel,
        grid=(S // ts,),
        in_specs=[
            pl.BlockSpec((ts, D), row),
            pl.BlockSpec((ts, 1), row),
            _const_spec((1, LANES)),
            _const_spec((1, D)),
            _const_spec(w.shape),
        ],
        out_specs=[pl.BlockSpec((ts, GROUP_COLS), row)] * 6,
        out_shape=[out] * 6,
        compiler_params=pltpu.CompilerParams(
            dimension_semantics=("arbitrary",), vmem_limit_bytes=VMEM_LIMIT),
        name="in_proj",
    )(x, pos_col, invf, g, w)


def _diff_attn_kernel(lam_ref, g_ref, qt_ref, k_ref, vt_ref, o_ref,
                      acc_ref, m_ref, sa_ref, sb_ref, *, lam_init):
    i = pl.program_id(1)
    acc_ref[...] = jnp.zeros_like(acc_ref)
    m_ref[...] = jnp.full_like(m_ref, NEG_BIG)

    def scores(j, s_ref):
        kb = k_ref[j]
        for c in range(2):
            s_ref[c] = jnp.dot(kb, qt_ref[c], preferred_element_type=F32)

    def accumulate(j, s_ref, diagonal):
        vb = vt_ref[j]
        for c in range(2):
            st = s_ref[c]
            if diagonal:
                key = lax.broadcasted_iota(jnp.int32, st.shape, 0)
                qry = lax.broadcasted_iota(jnp.int32, st.shape, 1)
                st = jnp.where(key <= qry, st, NEG_BIG)
            m_old = m_ref[c]
            m_new = jnp.maximum(m_old, jnp.max(st, axis=0, keepdims=True))
            alpha = jnp.exp2(m_old - m_new)
            p = jnp.exp2(st - m_new).astype(BF16)
            acc_ref[c] = alpha * acc_ref[c] + jnp.dot(vb, p, preferred_element_type=F32)
            m_ref[c] = m_new

    scores(0, sa_ref)

    def pair(t, carry):
        j = 2 * t
        scores(j + 1, sb_ref)
        accumulate(j, sa_ref, False)
        scores(j + 2, sa_ref)
        accumulate(j + 1, sb_ref, False)
        return carry

    lax.fori_loop(0, i // 2, pair, 0)

    @pl.when(i % 2 == 1)
    def _():
        scores(i, sb_ref)
        accumulate(i - 1, sa_ref, False)
        accumulate(i, sb_ref, True)

    @pl.when(i % 2 == 0)
    def _():
        accumulate(i, sa_ref, True)

    lam_v = lam_ref[...]
    lam = (jnp.exp(jnp.sum(lam_v[0:1] * lam_v[1:2], axis=1, keepdims=True))
           - jnp.exp(jnp.sum(lam_v[2:3] * lam_v[3:4], axis=1, keepdims=True)) + lam_init)
    a0 = acc_ref[0]
    a1 = acc_ref[1]
    o = a0[:DA_V] / a0[DA_V:DA_V + 1] - lam * (a1[:DA_V] / a1[DA_V:DA_V + 1])
    ms = jnp.mean(o * o, axis=0, keepdims=True)
    y = o * lax.rsqrt(ms + EPS) * g_ref[...] * (1.0 - lam_init)
    o_ref[...] = y.T.astype(o_ref.dtype)


def _diff_attn(lam_vecs, g_col, qt, kblk, vtblk, lam_init):
    H, _, _, S = qt.shape
    nk, tk = kblk.shape[1], kblk.shape[2]
    tq = tk
    return pl.pallas_call(
        functools.partial(_diff_attn_kernel, lam_init=lam_init),
        grid=(H, S // tq),
        in_specs=[
            _const_spec(lam_vecs.shape),
            _const_spec(g_col.shape),
            pl.BlockSpec((None, 2, LANES, tq), lambda h, i: (h, 0, 0, i)),
            pl.BlockSpec((None, nk, tk, LANES), lambda h, i: (h, 0, 0, 0)),
            pl.BlockSpec((None, nk, V_ROWS, tk), lambda h, i: (h, 0, 0, 0)),
        ],
        out_specs=pl.BlockSpec((tq, DA_V), lambda h, i: (i, h)),
        out_shape=jax.ShapeDtypeStruct((S, H * DA_V), BF16),
        scratch_shapes=[pltpu.VMEM((2, V_ROWS, tq), F32), pltpu.VMEM((2, 1, tq), F32),
                        pltpu.VMEM((2, tk, tq), F32), pltpu.VMEM((2, tk, tq), F32)],
        compiler_params=pltpu.CompilerParams(
            dimension_semantics=("arbitrary", "arbitrary"), vmem_limit_bytes=VMEM_LIMIT),
        name="diff_attn",
    )(lam_vecs, g_col, qt, kblk, vtblk)


def _sb_attn_kernel(q_ref, k_ref, v_ref, o_ref):
    i = pl.program_id(1)
    tq = q_ref.shape[0]
    tk = k_ref.shape[1]
    q2 = q_ref[...]
    lane = lax.broadcasted_iota(jnp.int32, q2.shape, 1)
    diff = (lax.broadcasted_iota(jnp.int32, (tq, tk), 1)
            - lax.broadcasted_iota(jnp.int32, (tq, tk), 0))
    later = (lax.broadcasted_iota(jnp.int32, (tk, tk), 0)
             > lax.broadcasted_iota(jnp.int32, (tk, tk), 1))
    tri = jnp.where(later, 1.0, 0.0).astype(BF16)

    def one_head(half):
        mine = (lane >= half * HEAD_DIM) & (lane < (half + 1) * HEAD_DIM)
        qh = jnp.where(mine, q2, jnp.zeros_like(q2))

        def cond(state):
            j, _, _, top = state
            return jnp.logical_and(j >= 0, top > SB_EXP_ZERO)

        def body(state):
            j, carry, acc, _ = state
            kb = k_ref[j]
            vb = v_ref[j]
            z = lax.dot_general(qh, kb, (((1,), (1,)), ((), ())),
                                preferred_element_type=F32)
            strict = diff < (i - j) * tq
            t = jnp.log1p(jnp.exp(-jnp.abs(z)))
            log_beta = -(jnp.maximum(-z, 0.0) + t)
            log_1mb = jnp.where(strict, -(jnp.maximum(z, 0.0) + t), 0.0)
            hi = log_1mb.astype(BF16)
            lo = (log_1mb - hi.astype(F32)).astype(BF16)
            between = (jnp.dot(hi, tri, preferred_element_type=F32)
                       + jnp.dot(lo, tri, preferred_element_type=F32) + carry)
            a = jnp.where(strict, jnp.exp(log_beta + between), 0.0).astype(BF16)
            acc = acc + jnp.dot(a, vb, preferred_element_type=F32)
            carry = carry + jnp.sum(log_1mb, axis=1, keepdims=True)
            return j - 1, carry, acc, jnp.max(carry)

        init = (i, jnp.zeros((tq, 1), F32), jnp.zeros((tq, LANES), F32), jnp.float32(0.0))
        return lax.while_loop(cond, body, init)[2]

    o = jnp.where(lane < HEAD_DIM, one_head(0), one_head(1))
    o_ref[...] = o.astype(o_ref.dtype)


def _sb_attn(q, kblk, vblk):
    S = q.shape[0]
    P, nk, tk, _ = kblk.shape
    tq = tk
    return pl.pallas_call(
        _sb_attn_kernel,
        grid=(P, S // tq),
        in_specs=[
            pl.BlockSpec((tq, LANES), lambda p, i: (i, p)),
            pl.BlockSpec((None, nk, tk, LANES), lambda p, i: (p, 0, 0, 0)),
            pl.BlockSpec((None, nk, tk, LANES), lambda p, i: (p, 0, 0, 0)),
        ],
        out_specs=pl.BlockSpec((tq, LANES), lambda p, i: (i, p)),
        out_shape=jax.ShapeDtypeStruct((S, P * LANES), BF16),
        compiler_params=pltpu.CompilerParams(
            dimension_semantics=("arbitrary", "arbitrary"), vmem_limit_bytes=VMEM_LIMIT),
        name="sb_attn",
    )(q, kblk, vblk)


def _mem_kv_kernel(mem_ref, g_ref, w_ref, kv_ref):
    mn = _rms(mem_ref[...], g_ref[...]).astype(BF16)
    kv_ref[...] = jnp.dot(mn, w_ref[...], preferred_element_type=F32).astype(kv_ref.dtype)


def _mem_kv(mem, g, w):
    M = mem.shape[0]
    return pl.pallas_call(
        _mem_kv_kernel,
        out_shape=jax.ShapeDtypeStruct((M, w.shape[1]), BF16),
        compiler_params=pltpu.CompilerParams(vmem_limit_bytes=VMEM_LIMIT),
        name="mem_kv",
    )(mem, g, w)


def _post_attn_kernel(x_ref, ya_ref, yb_ref, wout_ref, g_ref, wxq_ref, kv_ref, wxo_ref, h_ref):
    D = x_ref.shape[1]
    half = ya_ref.shape[1]
    h = (x_ref[...]
         + jnp.dot(ya_ref[...], wout_ref[:half], preferred_element_type=F32)
         + jnp.dot(yb_ref[...], wout_ref[half:], preferred_element_type=F32))
    hn = _rms(h, g_ref[...]).astype(BF16)
    xd = D // X_HEADS
    q = jnp.dot(hn, wxq_ref[...], preferred_element_type=F32) * (1.0 / math.sqrt(xd))
    q = q.astype(BF16)
    upd = jnp.zeros_like(h)
    for hd in range(X_HEADS):
        qh = q[:, hd * xd:(hd + 1) * xd]
        kh = kv_ref[:, hd * xd:(hd + 1) * xd]
        vh = kv_ref[:, D + hd * xd:D + (hd + 1) * xd]
        s = lax.dot_general(qh, kh, (((1,), (1,)), ((), ())), preferred_element_type=F32)
        p = jnp.exp(s - jnp.max(s, axis=-1, keepdims=True))
        l = jnp.sum(p, axis=-1, keepdims=True)
        oh = jnp.dot(p.astype(BF16), vh, preferred_element_type=F32) / l
        upd = upd + jnp.dot(oh.astype(BF16), wxo_ref[hd * xd:(hd + 1) * xd, :],
                            preferred_element_type=F32)
    h_ref[...] = h + upd


def _post_attn(x, ya, yb, wout, g, wxq, kv, wxo):
    S, D = x.shape
    ts = min(ROW_TILE, S)
    row = lambda i: (i, 0)
    return pl.pallas_call(
        _post_attn_kernel,
        grid=(S // ts,),
        in_specs=[
            pl.BlockSpec((ts, D), row),
            pl.BlockSpec((ts, ya.shape[1]), row),
            pl.BlockSpec((ts, yb.shape[1]), row),
            _const_spec(wout.shape),
            _const_spec(g.shape),
            _const_spec(wxq.shape),
            _const_spec(kv.shape),
            _const_spec(wxo.shape),
        ],
        out_specs=pl.BlockSpec((ts, D), row),
        out_shape=jax.ShapeDtypeStruct((S, D), F32),
        compiler_params=pltpu.CompilerParams(
            dimension_semantics=("arbitrary",), vmem_limit_bytes=VMEM_LIMIT),
        name="post_attn",
    )(x, ya, yb, wout, g, wxq, kv, wxo)


def _mlp_kernel(h_ref, g_ref, wup_ref, wdown_ref, gf_ref, o_ref, *, final_norm):
    h = h_ref[...]
    D = h.shape[1]
    hn = _rms(h, g_ref[...]).astype(BF16)
    upd = jnp.zeros_like(h)
    for c in range(wup_ref.shape[1] // D):
        a = jnp.dot(hn, wup_ref[:, c * D:(c + 1) * D], preferred_element_type=F32)
        a = jnp.square(jnp.maximum(a, 0.0)).astype(BF16)
        upd = upd + jnp.dot(a, wdown_ref[c * D:(c + 1) * D, :], preferred_element_type=F32)
    h = h + upd
    if final_norm:
        h = _rms(h, gf_ref[...])
    o_ref[...] = h


def _mlp(h, g, wup, wdown, gf, final_norm):
    S, D = h.shape
    ts = min(ROW_TILE, S)
    row = lambda i: (i, 0)
    return pl.pallas_call(
        functools.partial(_mlp_kernel, final_norm=final_norm),
        grid=(S // ts,),
        in_specs=[
            pl.BlockSpec((ts, D), row),
            _const_spec(g.shape),
            _const_spec(wup.shape),
            _const_spec(wdown.shape),
            _const_spec(gf.shape),
        ],
        out_specs=pl.BlockSpec((ts, D), row),
        out_shape=jax.ShapeDtypeStruct((S, D), F32),
        compiler_params=pltpu.CompilerParams(
            dimension_semantics=("arbitrary",), vmem_limit_bytes=VMEM_LIMIT),
        name="mlp",
    )(h, g, wup, wdown, gf)


def _rope_inv_freq_lanes():
    inv = np.float32(ROPE_THETA) ** (-np.arange(0, ROT_DIM, 2, dtype=np.float32) / np.float32(ROT_DIM))
    lanes = np.zeros((1, LANES), np.float32)
    for head in range(LANES // HEAD_DIM):
        base = head * HEAD_DIM
        lanes[0, base:base + ROT_DIM // 2] = inv
        lanes[0, base + ROT_DIM // 2:base + ROT_DIM] = inv
    return jnp.asarray(lanes)


def _layer(h, pos_col, invf, mem, p, lam_init, gf, final_norm):
    S, D = h.shape
    t = min(DIFF_TILE, S)
    nk = S // t
    qa, ka, va, qs, ks, vs = _in_proj(h, pos_col, invf, p["g_mix"], p["w_in"])

    q4 = qa.reshape(S, DA_HEADS, 2, HEAD_DIM).transpose(1, 2, 3, 0)
    zeros = jnp.zeros_like(q4[:, 0])
    qt = jnp.stack([jnp.concatenate([q4[:, 0], zeros], axis=1),
                    jnp.concatenate([zeros, q4[:, 1]], axis=1)], axis=1)
    kblk = ka.reshape(nk, t, DA_HEADS, LANES).transpose(2, 0, 1, 3)
    vt = va.reshape(nk, t, DA_HEADS, DA_V).transpose(2, 0, 3, 1)
    ones_pad = jnp.zeros((DA_HEADS, nk, V_ROWS - DA_V, t), BF16).at[:, :, 0, :].set(1.0)
    vtblk = jnp.concatenate([vt, ones_pad], axis=2)
    ya = _diff_attn(p["lam_vecs"], p["g_subln"], qt, kblk, vtblk, lam_init)

    pairs = SB_HEADS // 2
    t = min(SB_TILE, S)
    nk = S // t
    ksb = ks.reshape(nk, t, pairs, LANES).transpose(2, 0, 1, 3)
    vsb = vs.reshape(nk, t, pairs, LANES).transpose(2, 0, 1, 3)
    yb = _sb_attn(qs, ksb, vsb)

    kv = _mem_kv(mem, p["g_mem"], p["w_xkv"])
    h = _post_attn(h, ya, yb, p["w_out"], p["g_cross"], p["w_xq"], kv, p["w_xo"])
    return _mlp(h, p["g_mlp"], p["w_up"], p["w_down"], gf, final_norm)


def kernel(x, mem, positions, g_mix, w_in, lambda_q1, lambda_k1, lambda_q2, lambda_k2, g_subln, w_out, g_cross, g_mem, w_xq, w_xkv, w_xo, g_mlp, w_up, w_down, g_final):
    B, S, D = x.shape
    depth = w_in.shape[0]
    invf = _rope_inv_freq_lanes()
    gf = g_final.reshape(1, D).astype(F32)
    outs = []
    for b in range(B):
        h = x[b]
        pos_col = positions[b].reshape(S, 1)
        for l in range(depth):
            lam_init = 0.8 - 0.6 * math.exp(-0.3 * l)
            p = {
                "g_mix": g_mix[l].reshape(1, D),
                "w_in": w_in[l].astype(BF16),
                "lam_vecs": jnp.stack([lambda_q1[l], lambda_k1[l], lambda_q2[l], lambda_k2[l]]).astype(F32),
                "g_subln": g_subln[l].reshape(DA_V, 1).astype(F32),
                "w_out": w_out[l].astype(BF16),
                "g_cross": g_cross[l].reshape(1, D),
                "g_mem": g_mem[l].reshape(1, D),
                "w_xq": w_xq[l].astype(BF16),
                "w_xkv": w_xkv[l].astype(BF16),
                "w_xo": w_xo[l].astype(BF16),
                "g_mlp": g_mlp[l].reshape(1, D),
                "w_up": w_up[l].astype(BF16),
                "w_down": w_down[l].astype(BF16),
            }
            h = _layer(h, pos_col, invf, mem[b], p, lam_init, gf, l == depth - 1)
        outs.append(h)
    return jnp.stack(outs)
```

```python
import functools
import math

import numpy as np
import jax
import jax.numpy as jnp
from jax import lax
from jax.experimental import pallas as pl
from jax.experimental.pallas import tpu as pltpu

F32 = jnp.float32
BF16 = jnp.bfloat16

HEAD_DIM = 64
DA_HEADS = 4
DA_V = 2 * HEAD_DIM
SB_HEADS = 8
GROUP_COLS = 512
ROT_DIM = HEAD_DIM // 4
ROPE_THETA = 500000.0
X_HEADS = 4
EPS = 1e-6
LANES = 128
NEG_BIG = -1e30

SB_ZERO_LOG2 = 151.0

VMEM_LIMIT = 56 * 1024 * 1024

ROW_TILE = 512
DIFF_TILE = 512
SB_TILE = 256
V_ROWS = DA_V + 16


def _rms(x, g):
    ms = jnp.mean(x * x, axis=-1, keepdims=True)
    return x * lax.rsqrt(ms + EPS) * g


def _const_spec(shape):
    nd = len(shape)
    return pl.BlockSpec(shape, lambda *_: (0,) * nd, pipeline_mode=pl.Buffered(1))


def _in_proj_kernel(x_ref, pos_ref, invf_ref, g_ref, w_ref, wvt_ref,
                    qt_ref, ka_ref, vt_ref, qs_ref, ks_ref, vs_ref):
    ts = x_ref.shape[0]
    xn = _rms(x_ref[...], g_ref[...]).astype(BF16)

    ang = pos_ref[...].astype(F32) * invf_ref[...]
    cos = jnp.cos(ang)
    sin = jnp.sin(ang)
    j = lax.broadcasted_iota(jnp.int32, (1, LANES), 1) % HEAD_DIM
    s_up = jnp.where(j < ROT_DIM // 2, -sin, 0.0)
    s_dn = jnp.where(j >= ROT_DIM // 2, sin, 0.0)

    def group(idx):
        w = w_ref[:, idx * GROUP_COLS:(idx + 1) * GROUP_COLS]
        return jnp.dot(xn, w, preferred_element_type=F32)

    def rope(u, scale):
        parts = []
        for c in range(GROUP_COLS // LANES):
            uc = u[:, c * LANES:(c + 1) * LANES]
            r = (uc * cos + pltpu.roll(uc, LANES - ROT_DIM // 2, axis=1) * s_up
                 + pltpu.roll(uc, ROT_DIM // 2, axis=1) * s_dn)
            parts.append(r * scale)
        return jnp.concatenate(parts, axis=1)

    q_scale = math.log2(math.e) / math.sqrt(HEAD_DIM)

    qt = rope(group(0), q_scale).T
    zero_half = jnp.zeros((HEAD_DIM, ts), BF16)
    for hc in range(2 * DA_HEADS):
        c = hc % 2
        piece = qt[hc * HEAD_DIM:(hc + 1) * HEAD_DIM].astype(BF16)
        qt_ref[hc * LANES + c * HEAD_DIM:hc * LANES + (c + 1) * HEAD_DIM, :] = piece
        qt_ref[hc * LANES + (1 - c) * HEAD_DIM:hc * LANES + (2 - c) * HEAD_DIM, :] = zero_half

    ka_ref[...] = rope(group(1), 1.0).astype(BF16)

    vt = lax.dot_general(wvt_ref[...], xn, (((1,), (1,)), ((), ())),
                         preferred_element_type=F32)
    pad_row = lax.broadcasted_iota(jnp.int32, (V_ROWS - DA_V, ts), 0)
    ones_pad = jnp.where(pad_row == 0, 1.0, 0.0).astype(BF16)
    for h in range(DA_HEADS):
        vt_ref[0, h * V_ROWS:h * V_ROWS + DA_V, :] = vt[h * DA_V:(h + 1) * DA_V].astype(BF16)
        vt_ref[0, h * V_ROWS + DA_V:(h + 1) * V_ROWS, :] = ones_pad

    qs_ref[...] = (group(3) * q_scale).astype(BF16)
    ks_ref[...] = group(4).astype(BF16)
    vs_ref[...] = group(5).astype(BF16)


def _in_proj(x, pos_col, invf, g, w, wvt, ts):
    S, D = x.shape
    nt = S // ts
    flat = jax.ShapeDtypeStruct((S, GROUP_COLS), BF16)
    row = lambda i: (i, 0)
    flat_spec = pl.BlockSpec((ts, GROUP_COLS), row)
    return pl.pallas_call(
        _in_proj_kernel,
        grid=(nt,),
        in_specs=[
            pl.BlockSpec((ts, D), row),
            pl.BlockSpec((ts, 1), row),
            _const_spec((1, LANES)),
            _const_spec((1, D)),
            _const_spec(w.shape),
            _const_spec(wvt.shape),
        ],
        out_specs=[
            pl.BlockSpec((2 * DA_HEADS * LANES, ts), lambda i: (0, i)),
            flat_spec,
            pl.BlockSpec((1, DA_HEADS * V_ROWS, ts), lambda i: (i, 0, 0)),
            flat_spec, flat_spec, flat_spec,
        ],
        out_shape=[
            jax.ShapeDtypeStruct((2 * DA_HEADS * LANES, S), BF16),
            flat,
            jax.ShapeDtypeStruct((nt, DA_HEADS * V_ROWS, ts), BF16),
            flat, flat, flat,
        ],
        compiler_params=pltpu.CompilerParams(
            dimension_semantics=("arbitrary",), vmem_limit_bytes=VMEM_LIMIT),
        name="in_proj",
    )(x, pos_col, invf, g, w, wvt)


def _diff_attn_kernel(lam_ref, g_ref, qt_ref, k_ref, vt_ref, o_ref,
                      acc_ref, m_ref, sa_ref, sb_ref, *, lam_init):
    i = pl.program_id(1)
    t = qt_ref.shape[1]
    acc_ref[...] = jnp.zeros_like(acc_ref)
    m_ref[...] = jnp.full_like(m_ref, NEG_BIG)

    def scores(j, s_ref):
        kb = k_ref[pl.ds(pl.multiple_of(j * t, t), t), :]
        for c in range(2):
            s_ref[c] = jnp.dot(kb, qt_ref[c * LANES:(c + 1) * LANES, :],
                               preferred_element_type=F32)

    def accumulate(j, s_ref, diagonal):
        vb = vt_ref[j]
        for c in range(2):
            st = s_ref[c]
            if diagonal:
                key = lax.broadcasted_iota(jnp.int32, st.shape, 0)
                qry = lax.broadcasted_iota(jnp.int32, st.shape, 1)
                st = jnp.where(key <= qry, st, NEG_BIG)
            m_old = m_ref[c]
            m_new = jnp.maximum(m_old, jnp.max(st, axis=0, keepdims=True))
            alpha = jnp.exp2(m_old - m_new)
            p = jnp.exp2(st - m_new).astype(BF16)
            acc_ref[c] = alpha * acc_ref[c] + jnp.dot(vb, p, preferred_element_type=F32)
            m_ref[c] = m_new

    scores(0, sa_ref)

    def pair(n, carry):
        j = 2 * n
        scores(j + 1, sb_ref)
        accumulate(j, sa_ref, False)
        scores(j + 2, sa_ref)
        accumulate(j + 1, sb_ref, False)
        return carry

    lax.fori_loop(0, i // 2, pair, 0)

    @pl.when(i % 2 == 1)
    def _():
        scores(i, sb_ref)
        accumulate(i - 1, sa_ref, False)
        accumulate(i, sb_ref, True)

    @pl.when(i % 2 == 0)
    def _():
        accumulate(i, sa_ref, True)

    lam_v = lam_ref[...]
    lam = (jnp.exp(jnp.sum(lam_v[0:1] * lam_v[1:2], axis=1, keepdims=True))
           - jnp.exp(jnp.sum(lam_v[2:3] * lam_v[3:4], axis=1, keepdims=True)) + lam_init)
    a0 = acc_ref[0]
    a1 = acc_ref[1]
    o = a0[:DA_V] / a0[DA_V:DA_V + 1] - lam * (a1[:DA_V] / a1[DA_V:DA_V + 1])
    ms = jnp.mean(o * o, axis=0, keepdims=True)
    y = o * lax.rsqrt(ms + EPS) * g_ref[...] * (1.0 - lam_init)
    o_ref[...] = y.T.astype(o_ref.dtype)


def _diff_attn(lam_vecs, g_col, qt, ka, vtblk, lam_init):
    S = ka.shape[0]
    nk, _, t = vtblk.shape
    return pl.pallas_call(
        functools.partial(_diff_attn_kernel, lam_init=lam_init),
        grid=(DA_HEADS, nk),
        in_specs=[
            _const_spec(lam_vecs.shape),
            _const_spec(g_col.shape),
            pl.BlockSpec((2 * LANES, t), lambda h, i: (h, i)),
            pl.BlockSpec((S, LANES), lambda h, i: (0, h)),
            pl.BlockSpec((nk, V_ROWS, t), lambda h, i: (0, h, 0)),
        ],
        out_specs=pl.BlockSpec((t, DA_V), lambda h, i: (i, h)),
        out_shape=jax.ShapeDtypeStruct((S, DA_HEADS * DA_V), BF16),
        scratch_shapes=[pltpu.VMEM((2, V_ROWS, t), F32), pltpu.VMEM((2, 1, t), F32),
                        pltpu.VMEM((2, t, t), F32), pltpu.VMEM((2, t, t), F32)],
        compiler_params=pltpu.CompilerParams(
            dimension_semantics=("arbitrary", "arbitrary"), vmem_limit_bytes=VMEM_LIMIT),
        name="diff_attn",
    )(lam_vecs, g_col, qt, ka, vtblk)


def _sb_attn_kernel(q_ref, k_ref, v_ref, o_ref):
    i = pl.program_id(1)
    t = q_ref.shape[0]
    q2 = q_ref[...]
    lane = lax.broadcasted_iota(jnp.int32, q2.shape, 1)
    zeros = jnp.zeros_like(q2)
    qh = (jnp.where(lane < HEAD_DIM, q2, zeros), jnp.where(lane >= HEAD_DIM, q2, zeros))
    row = lax.broadcasted_iota(jnp.int32, (t, t), 0)
    col = lax.broadcasted_iota(jnp.int32, (t, t), 1)
    tri = jnp.where(row > col, 1.0, 0.0).astype(BF16)

    def block(j, head, carry, acc, diagonal):
        rows = pl.ds(pl.multiple_of(j * t, t), t)
        kb = k_ref[rows, :]
        vb = v_ref[rows, :]
        z = lax.dot_general(qh[head], kb, (((1,), (1,)), ((), ())),
                            preferred_element_type=F32)
        sp = jnp.maximum(z, 0.0) + jnp.log2(1.0 + jnp.exp2(-jnp.abs(z)))
        if diagonal:
            sp = jnp.where(col < row, sp, 0.0)
        hi = sp.astype(BF16)
        lo = (sp - hi.astype(F32)).astype(BF16)
        between = (jnp.dot(hi, tri, preferred_element_type=F32)
                   + jnp.dot(lo, tri, preferred_element_type=F32) + carry)
        a = jnp.exp2(z - sp - between)
        if diagonal:
            a = jnp.where(col < row, a, 0.0)
        acc = acc + jnp.dot(a.astype(BF16), vb, preferred_element_type=F32)
        carry = carry + jnp.sum(sp, axis=1, keepdims=True)
        return carry, acc

    def lowest(c0, c1):
        return jnp.minimum(jnp.min(c0), jnp.min(c1))

    carry0 = jnp.zeros((t, 1), F32)
    acc0 = jnp.zeros((t, LANES), F32)
    c0, a0 = block(i, 0, carry0, acc0, True)
    c1, a1 = block(i, 1, carry0, acc0, True)

    def cond(state):
        return jnp.logical_and(state[0] >= 0, state[5] < SB_ZERO_LOG2)

    def body(state):
        j, c0, a0, c1, a1, _ = state
        c0, a0 = block(j, 0, c0, a0, False)
        c1, a1 = block(j, 1, c1, a1, False)
        return j - 1, c0, a0, c1, a1, lowest(c0, c1)

    state = lax.while_loop(cond, body, (i - 1, c0, a0, c1, a1, lowest(c0, c1)))
    o_ref[...] = jnp.where(lane < HEAD_DIM, state[2], state[4]).astype(o_ref.dtype)


def _sb_attn(q, k, v, t):
    S = q.shape[0]
    pairs = SB_HEADS // 2
    return pl.pallas_call(
        _sb_attn_kernel,
        grid=(pairs, S // t),
        in_specs=[
            pl.BlockSpec((t, LANES), lambda p, i: (i, p)),
            pl.BlockSpec((S, LANES), lambda p, i: (0, p)),
            pl.BlockSpec((S, LANES), lambda p, i: (0, p)),
        ],
        out_specs=pl.BlockSpec((t, LANES), lambda p, i: (i, p)),
        out_shape=jax.ShapeDtypeStruct((S, pairs * LANES), BF16),
        compiler_params=pltpu.CompilerParams(
            dimension_semantics=("arbitrary", "arbitrary"), vmem_limit_bytes=VMEM_LIMIT),
        name="sb_attn",
    )(q, k, v)


def _mem_kv_kernel(mem_ref, g_ref, w_ref, kv_ref):
    mn = _rms(mem_ref[...], g_ref[...]).astype(BF16)
    kv_ref[...] = jnp.dot(mn, w_ref[...], preferred_element_type=F32).astype(kv_ref.dtype)


def _mem_kv(mem, g, w):
    M = mem.shape[0]
    return pl.pallas_call(
        _mem_kv_kernel,
        out_shape=jax.ShapeDtypeStruct((M, w.shape[1]), BF16),
        compiler_params=pltpu.CompilerParams(vmem_limit_bytes=VMEM_LIMIT),
        name="mem_kv",
    )(mem, g, w)


def _post_attn_kernel(x_ref, ya_ref, yb_ref, wout_ref, g_ref, wxq_ref, kv_ref, wxo_ref, h_ref):
    D = x_ref.shape[1]
    half = ya_ref.shape[1]
    h = (x_ref[...]
         + jnp.dot(ya_ref[...], wout_ref[:half], preferred_element_type=F32)
         + jnp.dot(yb_ref[...], wout_ref[half:], preferred_element_type=F32))
    hn = _rms(h, g_ref[...]).astype(BF16)
    xd = D // X_HEADS
    q = jnp.dot(hn, wxq_ref[...], preferred_element_type=F32) * (1.0 / math.sqrt(xd))
    q = q.astype(BF16)
    upd = jnp.zeros_like(h)
    for hd in range(X_HEADS):
        qh = q[:, hd * xd:(hd + 1) * xd]
        kh = kv_ref[:, hd * xd:(hd + 1) * xd]
        vh = kv_ref[:, D + hd * xd:D + (hd + 1) * xd]
        s = lax.dot_general(qh, kh, (((1,), (1,)), ((), ())), preferred_element_type=F32)
        p = jnp.exp(s - jnp.max(s, axis=-1, keepdims=True))
        l = jnp.sum(p, axis=-1, keepdims=True)
        oh = jnp.dot(p.astype(BF16), vh, preferred_element_type=F32) / l
        upd = upd + jnp.dot(oh.astype(BF16), wxo_ref[hd * xd:(hd + 1) * xd, :],
                            preferred_element_type=F32)
    h_ref[...] = h + upd


def _post_attn(x, ya, yb, wout, g, wxq, kv, wxo):
    S, D = x.shape
    ts = min(ROW_TILE, S)
    row = lambda i: (i, 0)
    return pl.pallas_call(
        _post_attn_kernel,
        grid=(S // ts,),
        in_specs=[
            pl.BlockSpec((ts, D), row),
            pl.BlockSpec((ts, ya.shape[1]), row),
            pl.BlockSpec((ts, yb.shape[1]), row),
            _const_spec(wout.shape),
            _const_spec(g.shape),
            _const_spec(wxq.shape),
            _const_spec(kv.shape),
            _const_spec(wxo.shape),
        ],
        out_specs=pl.BlockSpec((ts, D), row),
        out_shape=jax.ShapeDtypeStruct((S, D), F32),
        compiler_params=pltpu.CompilerParams(
            dimension_semantics=("arbitrary",), vmem_limit_bytes=VMEM_LIMIT),
        name="post_attn",
    )(x, ya, yb, wout, g, wxq, kv, wxo)


def _mlp_kernel(h_ref, g_ref, wup_ref, wdown_ref, gf_ref, o_ref, *, final_norm):
    h = h_ref[...]
    D = h.shape[1]
    hn = _rms(h, g_ref[...]).astype(BF16)
    upd = jnp.zeros_like(h)
    for c in range(wup_ref.shape[1] // D):
        a = jnp.dot(hn, wup_ref[:, c * D:(c + 1) * D], preferred_element_type=F32)
        a = jnp.square(jnp.maximum(a, 0.0)).astype(BF16)
        upd = upd + jnp.dot(a, wdown_ref[c * D:(c + 1) * D, :], preferred_element_type=F32)
    h = h + upd
    if final_norm:
        h = _rms(h, gf_ref[...])
    o_ref[...] = h


def _mlp(h, g, wup, wdown, gf, final_norm):
    S, D = h.shape
    ts = min(ROW_TILE, S)
    row = lambda i: (i, 0)
    return pl.pallas_call(
        functools.partial(_mlp_kernel, final_norm=final_norm),
        grid=(S // ts,),
        in_specs=[
            pl.BlockSpec((ts, D), row),
            _const_spec(g.shape),
            _const_spec(wup.shape),
            _const_spec(wdown.shape),
            _const_spec(gf.shape),
        ],
        out_specs=pl.BlockSpec((ts, D), row),
        out_shape=jax.ShapeDtypeStruct((S, D), F32),
        compiler_params=pltpu.CompilerParams(
            dimension_semantics=("arbitrary",), vmem_limit_bytes=VMEM_LIMIT),
        name="mlp",
    )(h, g, wup, wdown, gf)


def _rope_inv_freq_lanes():
    inv = np.float32(ROPE_THETA) ** (-np.arange(0, ROT_DIM, 2, dtype=np.float32) / np.float32(ROT_DIM))
    lanes = np.zeros((1, LANES), np.float32)
    for head in range(LANES // HEAD_DIM):
        base = head * HEAD_DIM
        lanes[0, base:base + ROT_DIM // 2] = inv
        lanes[0, base + ROT_DIM // 2:base + ROT_DIM] = inv
    return jnp.asarray(lanes)


def _layer(h, pos_col, invf, mem, p, lam_init, gf, final_norm):
    S, D = h.shape
    td = min(DIFF_TILE, S)
    qt, ka, vtblk, qs, ks, vs = _in_proj(h, pos_col, invf, p["g_mix"], p["w_in"], p["w_va_t"], td)
    ya = _diff_attn(p["lam_vecs"], p["g_subln"], qt, ka, vtblk, lam_init)
    yb = _sb_attn(qs, ks, vs, min(SB_TILE, S))
    kv = _mem_kv(mem, p["g_mem"], p["w_xkv"])
    h = _post_attn(h, ya, yb, p["w_out"], p["g_cross"], p["w_xq"], kv, p["w_xo"])
    return _mlp(h, p["g_mlp"], p["w_up"], p["w_down"], gf, final_norm)


def kernel(x, mem, positions, g_mix, w_in, lambda_q1, lambda_k1, lambda_q2, lambda_k2, g_subln, w_out, g_cross, g_mem, w_xq, w_xkv, w_xo, g_mlp, w_up, w_down, g_final):
    B, S, D = x.shape
    depth = w_in.shape[0]
    invf = _rope_inv_freq_lanes()
    gf = g_final.reshape(1, D).astype(F32)
    outs = []
    for b in range(B):
        h = x[b]
        pos_col = positions[b].reshape(S, 1)
        for l in range(depth):
            lam_init = 0.8 - 0.6 * math.exp(-0.3 * l)
            p = {
                "g_mix": g_mix[l].reshape(1, D),
                "w_in": w_in[l].astype(BF16),
                "w_va_t": w_in[l][:, 2 * GROUP_COLS:3 * GROUP_COLS].T.astype(BF16),
                "lam_vecs": jnp.stack([lambda_q1[l], lambda_k1[l], lambda_q2[l], lambda_k2[l]]).astype(F32),
                "g_subln": g_subln[l].reshape(DA_V, 1).astype(F32),
                "w_out": w_out[l].astype(BF16),
                "g_cross": g_cross[l].reshape(1, D),
                "g_mem": g_mem[l].reshape(1, D),
                "w_xq": w_xq[l].astype(BF16),
                "w_xkv": w_xkv[l].astype(BF16),
                "w_xo": w_xo[l].astype(BF16),
                "g_mlp": g_mlp[l].reshape(1, D),
                "w_up": w_up[l].astype(BF16),
                "w_down": w_down[l].astype(BF16),
            }
            h = _layer(h, pos_col, invf, mem[b], p, lam_init, gf, l == depth - 1)
        outs.append(h)
    return jnp.stack(outs)
```

```python
import functools
import math

import numpy as np
import jax
import jax.numpy as jnp
from jax import lax
from jax.experimental import pallas as pl
from jax.experimental.pallas import tpu as pltpu

F32 = jnp.float32
BF16 = jnp.bfloat16

HEAD_DIM = 64
DA_HEADS = 4
DA_V = 2 * HEAD_DIM
SB_HEADS = 8
GROUP_COLS = 512
ROT_DIM = HEAD_DIM // 4
ROPE_THETA = 500000.0
X_HEADS = 4
EPS = 1e-6
LANES = 128
NEG_BIG = -1e30

SB_ZERO_LOG2 = 151.0

VMEM_LIMIT = 56 * 1024 * 1024

ROW_TILE = 512
DIFF_TILE = 512
SB_TILE = 256
V_ROWS = DA_V + 16


def _rms(x, g):
    ms = jnp.mean(x * x, axis=-1, keepdims=True)
    return x * lax.rsqrt(ms + EPS) * g


def _const_spec(shape):
    nd = len(shape)
    return pl.BlockSpec(shape, lambda *_: (0,) * nd, pipeline_mode=pl.Buffered(1))


def _in_proj_kernel(x_ref, pos_ref, invf_ref, g_ref, w_ref, wvt_ref,
                    qt_ref, ka_ref, vt_ref, qs_ref, ks_ref, vs_ref):
    ts = x_ref.shape[0]
    xn = _rms(x_ref[...], g_ref[...]).astype(BF16)

    ang = pos_ref[...].astype(F32) * invf_ref[...]
    cos = jnp.cos(ang)
    sin = jnp.sin(ang)
    j = lax.broadcasted_iota(jnp.int32, (1, LANES), 1) % HEAD_DIM
    s_up = jnp.where(j < ROT_DIM // 2, -sin, 0.0)
    s_dn = jnp.where(j >= ROT_DIM // 2, sin, 0.0)

    def group(idx):
        w = w_ref[:, idx * GROUP_COLS:(idx + 1) * GROUP_COLS]
        return jnp.dot(xn, w, preferred_element_type=F32)

    def rope(u, scale):
        parts = []
        for c in range(GROUP_COLS // LANES):
            uc = u[:, c * LANES:(c + 1) * LANES]
            r = (uc * cos + pltpu.roll(uc, LANES - ROT_DIM // 2, axis=1) * s_up
                 + pltpu.roll(uc, ROT_DIM // 2, axis=1) * s_dn)
            parts.append(r * scale)
        return jnp.concatenate(parts, axis=1)

    q_scale = math.log2(math.e) / math.sqrt(HEAD_DIM)

    qt = rope(group(0), q_scale).T
    zero_half = jnp.zeros((HEAD_DIM, ts), BF16)
    for hc in range(2 * DA_HEADS):
        c = hc % 2
        piece = qt[hc * HEAD_DIM:(hc + 1) * HEAD_DIM].astype(BF16)
        qt_ref[hc * LANES + c * HEAD_DIM:hc * LANES + (c + 1) * HEAD_DIM, :] = piece
        qt_ref[hc * LANES + (1 - c) * HEAD_DIM:hc * LANES + (2 - c) * HEAD_DIM, :] = zero_half

    ka_ref[...] = rope(group(1), 1.0).astype(BF16)

    vt = lax.dot_general(wvt_ref[...], xn, (((1,), (1,)), ((), ())),
                         preferred_element_type=F32)
    pad_row = lax.broadcasted_iota(jnp.int32, (V_ROWS - DA_V, ts), 0)
    ones_pad = jnp.where(pad_row == 0, 1.0, 0.0).astype(BF16)
    for h in range(DA_HEADS):
        vt_ref[0, h * V_ROWS:h * V_ROWS + DA_V, :] = vt[h * DA_V:(h + 1) * DA_V].astype(BF16)
        vt_ref[0, h * V_ROWS + DA_V:(h + 1) * V_ROWS, :] = ones_pad

    qs_ref[...] = (group(3) * q_scale).astype(BF16)
    ks_ref[...] = group(4).astype(BF16)
    vs_ref[...] = group(5).astype(BF16)


def _in_proj(x, pos_col, invf, g, w, wvt, ts):
    S, D = x.shape
    nt = S // ts
    flat = jax.ShapeDtypeStruct((S, GROUP_COLS), BF16)
    row = lambda i: (i, 0)
    flat_spec = pl.BlockSpec((ts, GROUP_COLS), row)
    return pl.pallas_call(
        _in_proj_kernel,
        grid=(nt,),
        in_specs=[
            pl.BlockSpec((ts, D), row),
            pl.BlockSpec((ts, 1), row),
            _const_spec((1, LANES)),
            _const_spec((1, D)),
            _const_spec(w.shape),
            _const_spec(wvt.shape),
        ],
        out_specs=[
            pl.BlockSpec((2 * DA_HEADS * LANES, ts), lambda i: (0, i)),
            flat_spec,
            pl.BlockSpec((1, DA_HEADS * V_ROWS, ts), lambda i: (i, 0, 0)),
            flat_spec, flat_spec, flat_spec,
        ],
        out_shape=[
            jax.ShapeDtypeStruct((2 * DA_HEADS * LANES, S), BF16),
            flat,
            jax.ShapeDtypeStruct((nt, DA_HEADS * V_ROWS, ts), BF16),
            flat, flat, flat,
        ],
        compiler_params=pltpu.CompilerParams(
            dimension_semantics=("arbitrary",), vmem_limit_bytes=VMEM_LIMIT),
        name="in_proj",
    )(x, pos_col, invf, g, w, wvt)


def _diff_attn_kernel(lam_ref, g_ref, qt_ref, k_ref, vt_ref, o_ref,
                      acc_ref, m_ref, sa_ref, sb_ref, ma_ref, mb_ref, *, lam_init):
    i = pl.program_id(1)
    t = qt_ref.shape[1]
    acc_ref[...] = jnp.zeros_like(acc_ref)
    m_ref[...] = jnp.full_like(m_ref, NEG_BIG)

    def scores(j, s_ref, top_ref):
        kb = k_ref[pl.ds(pl.multiple_of(j * t, t), t), :]
        for c in range(2):
            st = jnp.dot(kb, qt_ref[c * LANES:(c + 1) * LANES, :], preferred_element_type=F32)
            s_ref[c] = st
            top_ref[c] = jnp.max(st, axis=0, keepdims=True)

    def accumulate(j, s_ref, top_ref, diagonal):
        vb = vt_ref[j]
        for c in range(2):
            st = s_ref[c]
            if diagonal:
                key = lax.broadcasted_iota(jnp.int32, st.shape, 0)
                qry = lax.broadcasted_iota(jnp.int32, st.shape, 1)
                st = jnp.where(key <= qry, st, NEG_BIG)
                top = jnp.max(st, axis=0, keepdims=True)
            else:
                top = top_ref[c]
            m_old = m_ref[c]
            m_new = jnp.maximum(m_old, top)
            alpha = jnp.exp2(m_old - m_new)
            p = jnp.exp2(st - m_new).astype(BF16)
            acc_ref[c] = alpha * acc_ref[c] + jnp.dot(vb, p, preferred_element_type=F32)
            m_ref[c] = m_new

    scores(0, sa_ref, ma_ref)

    def pair(n, carry):
        j = 2 * n
        scores(j + 1, sb_ref, mb_ref)
        accumulate(j, sa_ref, ma_ref, False)
        scores(j + 2, sa_ref, ma_ref)
        accumulate(j + 1, sb_ref, mb_ref, False)
        return carry

    lax.fori_loop(0, i // 2, pair, 0)

    @pl.when(i % 2 == 1)
    def _():
        scores(i, sb_ref, mb_ref)
        accumulate(i - 1, sa_ref, ma_ref, False)
        accumulate(i, sb_ref, mb_ref, True)

    @pl.when(i % 2 == 0)
    def _():
        accumulate(i, sa_ref, ma_ref, True)

    lam_v = lam_ref[...]
    lam = (jnp.exp(jnp.sum(lam_v[0:1] * lam_v[1:2], axis=1, keepdims=True))
           - jnp.exp(jnp.sum(lam_v[2:3] * lam_v[3:4], axis=1, keepdims=True)) + lam_init)
    a0 = acc_ref[0]
    a1 = acc_ref[1]
    o = a0[:DA_V] / a0[DA_V:DA_V + 1] - lam * (a1[:DA_V] / a1[DA_V:DA_V + 1])
    ms = jnp.mean(o * o, axis=0, keepdims=True)
    y = o * lax.rsqrt(ms + EPS) * g_ref[...] * (1.0 - lam_init)
    o_ref[...] = y.T.astype(o_ref.dtype)


def _diff_attn(lam_vecs, g_col, qt, ka, vtblk, lam_init):
    S = ka.shape[0]
    nk, _, t = vtblk.shape
    return pl.pallas_call(
        functools.partial(_diff_attn_kernel, lam_init=lam_init),
        grid=(DA_HEADS, nk),
        in_specs=[
            _const_spec(lam_vecs.shape),
            _const_spec(g_col.shape),
            pl.BlockSpec((2 * LANES, t), lambda h, i: (h, i)),
            pl.BlockSpec((S, LANES), lambda h, i: (0, h)),
            pl.BlockSpec((nk, V_ROWS, t), lambda h, i: (0, h, 0)),
        ],
        out_specs=pl.BlockSpec((t, DA_V), lambda h, i: (i, h)),
        out_shape=jax.ShapeDtypeStruct((S, DA_HEADS * DA_V), BF16),
        scratch_shapes=[pltpu.VMEM((2, V_ROWS, t), F32), pltpu.VMEM((2, 1, t), F32),
                        pltpu.VMEM((2, t, t), F32), pltpu.VMEM((2, t, t), F32),
                        pltpu.VMEM((2, 1, t), F32), pltpu.VMEM((2, 1, t), F32)],
        compiler_params=pltpu.CompilerParams(
            dimension_semantics=("arbitrary", "arbitrary"), vmem_limit_bytes=VMEM_LIMIT),
        name="diff_attn",
    )(lam_vecs, g_col, qt, ka, vtblk)


def _sb_attn_kernel(q_ref, k_ref, v_ref, o_ref, acc_ref, carry_ref):
    i = pl.program_id(0)
    t = q_ref.shape[0]
    pairs = q_ref.shape[1] // LANES
    lane = lax.broadcasted_iota(jnp.int32, (t, LANES), 1)
    first = lane < HEAD_DIM
    row = lax.broadcasted_iota(jnp.int32, (t, t), 0)
    col = lax.broadcasted_iota(jnp.int32, (t, t), 1)
    tri = jnp.where(row > col, 1.0, 0.0).astype(BF16)
    strict = jnp.concatenate([col < row, col < row], axis=0)

    def block(j, p, diagonal):
        lanes = slice(p * LANES, (p + 1) * LANES)
        q2 = q_ref[:, lanes]
        zeros = jnp.zeros_like(q2)
        qs = jnp.concatenate([jnp.where(first, q2, zeros), jnp.where(first, zeros, q2)], axis=0)
        rows = pl.ds(pl.multiple_of(j * t, t), t)
        kb = k_ref[rows, lanes]
        vb = v_ref[rows, lanes]
        z = lax.dot_general(qs, kb, (((1,), (1,)), ((), ())),
                            preferred_element_type=F32)
        sp = jnp.maximum(z, 0.0) + jnp.log2(1.0 + jnp.exp2(-jnp.abs(z)))
        if diagonal:
            sp = jnp.where(strict, sp, 0.0)
            carry = jnp.zeros((2 * t, 1), F32)
        else:
            carry = carry_ref[p]
        hi = sp.astype(BF16)
        lo = (sp - hi.astype(F32)).astype(BF16)
        sums = jnp.dot(jnp.concatenate([hi, lo], axis=0), tri, preferred_element_type=F32)
        between = sums[:2 * t] + sums[2 * t:] + carry
        a = jnp.exp2(z - sp - between)
        if diagonal:
            a = jnp.where(strict, a, 0.0)
        av = jnp.dot(a.astype(BF16), vb, preferred_element_type=F32)
        acc_ref[p] = av if diagonal else acc_ref[p] + av
        carry = carry + jnp.sum(sp, axis=1, keepdims=True)
        carry_ref[p] = carry
        return carry

    def all_heads(j, diagonal):
        low = block(j, 0, diagonal)
        for p in range(1, pairs):
            low = jnp.minimum(low, block(j, p, diagonal))
        return jnp.min(low)

    def cond(state):
        return jnp.logical_and(state[0] >= 0, state[1] < SB_ZERO_LOG2)

    def body(state):
        j = state[0]
        return j - 1, all_heads(j, False)

    lax.while_loop(cond, body, (i - 1, all_heads(i, True)))
    for p in range(pairs):
        acc = acc_ref[p]
        o_ref[:, p * LANES:(p + 1) * LANES] = jnp.where(first, acc[:t], acc[t:]).astype(o_ref.dtype)


def _sb_attn(q, k, v, t):
    S, cols = q.shape
    pairs = cols // LANES
    return pl.pallas_call(
        _sb_attn_kernel,
        grid=(S // t,),
        in_specs=[
            pl.BlockSpec((t, cols), lambda i: (i, 0)),
            _const_spec(k.shape),
            _const_spec(v.shape),
        ],
        out_specs=pl.BlockSpec((t, cols), lambda i: (i, 0)),
        out_shape=jax.ShapeDtypeStruct((S, cols), BF16),
        scratch_shapes=[pltpu.VMEM((pairs, 2 * t, LANES), F32), pltpu.VMEM((pairs, 2 * t, 1), F32)],
        compiler_params=pltpu.CompilerParams(
            dimension_semantics=("arbitrary",), vmem_limit_bytes=VMEM_LIMIT),
        name="sb_attn",
    )(q, k, v)


def _mem_kv_kernel(mem_ref, g_ref, w_ref, kv_ref):
    mn = _rms(mem_ref[...], g_ref[...]).astype(BF16)
    kv_ref[...] = jnp.dot(mn, w_ref[...], preferred_element_type=F32).astype(kv_ref.dtype)


def _mem_kv(mem, g, w):
    M = mem.shape[0]
    return pl.pallas_call(
        _mem_kv_kernel,
        out_shape=jax.ShapeDtypeStruct((M, w.shape[1]), BF16),
        compiler_params=pltpu.CompilerParams(vmem_limit_bytes=VMEM_LIMIT),
        name="mem_kv",
    )(mem, g, w)


def _post_attn_kernel(x_ref, ya_ref, yb_ref, wout_ref, g_ref, wxq_ref, kv_ref, wxo_ref, h_ref):
    D = x_ref.shape[1]
    half = ya_ref.shape[1]
    h = (x_ref[...]
         + jnp.dot(ya_ref[...], wout_ref[:half], preferred_element_type=F32)
         + jnp.dot(yb_ref[...], wout_ref[half:], preferred_element_type=F32))
    hn = _rms(h, g_ref[...]).astype(BF16)
    xd = D // X_HEADS
    q = jnp.dot(hn, wxq_ref[...], preferred_element_type=F32) * (1.0 / math.sqrt(xd))
    q = q.astype(BF16)
    upd = jnp.zeros_like(h)
    for hd in range(X_HEADS):
        qh = q[:, hd * xd:(hd + 1) * xd]
        kh = kv_ref[:, hd * xd:(hd + 1) * xd]
        vh = kv_ref[:, D + hd * xd:D + (hd + 1) * xd]
        s = lax.dot_general(qh, kh, (((1,), (1,)), ((), ())), preferred_element_type=F32)
        p = jnp.exp(s - jnp.max(s, axis=-1, keepdims=True))
        l = jnp.sum(p, axis=-1, keepdims=True)
        oh = jnp.dot(p.astype(BF16), vh, preferred_element_type=F32) / l
        upd = upd + jnp.dot(oh.astype(BF16), wxo_ref[hd * xd:(hd + 1) * xd, :],
                            preferred_element_type=F32)
    h_ref[...] = h + upd


def _post_attn(x, ya, yb, wout, g, wxq, kv, wxo):
    S, D = x.shape
    ts = min(ROW_TILE, S)
    row = lambda i: (i, 0)
    return pl.pallas_call(
        _post_attn_kernel,
        grid=(S // ts,),
        in_specs=[
            pl.BlockSpec((ts, D), row),
            pl.BlockSpec((ts, ya.shape[1]), row),
            pl.BlockSpec((ts, yb.shape[1]), row),
            _const_spec(wout.shape),
            _const_spec(g.shape),
            _const_spec(wxq.shape),
            _const_spec(kv.shape),
            _const_spec(wxo.shape),
        ],
        out_specs=pl.BlockSpec((ts, D), row),
        out_shape=jax.ShapeDtypeStruct((S, D), F32),
        compiler_params=pltpu.CompilerParams(
            dimension_semantics=("arbitrary",), vmem_limit_bytes=VMEM_LIMIT),
        name="post_attn",
    )(x, ya, yb, wout, g, wxq, kv, wxo)


def _mlp_kernel(h_ref, g_ref, wup_ref, wdown_ref, gf_ref, o_ref, *, final_norm):
    h = h_ref[...]
    D = h.shape[1]
    hn = _rms(h, g_ref[...]).astype(BF16)
    upd = jnp.zeros_like(h)
    for c in range(wup_ref.shape[1] // D):
        a = jnp.dot(hn, wup_ref[:, c * D:(c + 1) * D], preferred_element_type=F32)
        a = jnp.square(jnp.maximum(a, 0.0)).astype(BF16)
        upd = upd + jnp.dot(a, wdown_ref[c * D:(c + 1) * D, :], preferred_element_type=F32)
    h = h + upd
    if final_norm:
        h = _rms(h, gf_ref[...])
    o_ref[...] = h


def _mlp(h, g, wup, wdown, gf, final_norm):
    S, D = h.shape
    ts = min(ROW_TILE, S)
    row = lambda i: (i, 0)
    return pl.pallas_call(
        functools.partial(_mlp_kernel, final_norm=final_norm),
        grid=(S // ts,),
        in_specs=[
            pl.BlockSpec((ts, D), row),
            _const_spec(g.shape),
            _const_spec(wup.shape),
            _const_spec(wdown.shape),
            _const_spec(gf.shape),
        ],
        out_specs=pl.BlockSpec((ts, D), row),
        out_shape=jax.ShapeDtypeStruct((S, D), F32),
        compiler_params=pltpu.CompilerParams(
            dimension_semantics=("arbitrary",), vmem_limit_bytes=VMEM_LIMIT),
        name="mlp",
    )(h, g, wup, wdown, gf)


def _rope_inv_freq_lanes():
    inv = np.float32(ROPE_THETA) ** (-np.arange(0, ROT_DIM, 2, dtype=np.float32) / np.float32(ROT_DIM))
    lanes = np.zeros((1, LANES), np.float32)
    for head in range(LANES // HEAD_DIM):
        base = head * HEAD_DIM
        lanes[0, base:base + ROT_DIM // 2] = inv
        lanes[0, base + ROT_DIM // 2:base + ROT_DIM] = inv
    return jnp.asarray(lanes)


def _layer(h, pos_col, invf, mem, p, lam_init, gf, final_norm):
    S, D = h.shape
    td = min(DIFF_TILE, S)
    qt, ka, vtblk, qs, ks, vs = _in_proj(h, pos_col, invf, p["g_mix"], p["w_in"], p["w_va_t"], td)
    ya = _diff_attn(p["lam_vecs"], p["g_subln"], qt, ka, vtblk, lam_init)
    yb = _sb_attn(qs, ks, vs, min(SB_TILE, S))
    kv = _mem_kv(mem, p["g_mem"], p["w_xkv"])
    h = _post_attn(h, ya, yb, p["w_out"], p["g_cross"], p["w_xq"], kv, p["w_xo"])
    return _mlp(h, p["g_mlp"], p["w_up"], p["w_down"], gf, final_norm)


def kernel(x, mem, positions, g_mix, w_in, lambda_q1, lambda_k1, lambda_q2, lambda_k2, g_subln, w_out, g_cross, g_mem, w_xq, w_xkv, w_xo, g_mlp, w_up, w_down, g_final):
    B, S, D = x.shape
    depth = w_in.shape[0]
    invf = _rope_inv_freq_lanes()
    gf = g_final.reshape(1, D).astype(F32)
    outs = []
    for b in range(B):
        h = x[b]
        pos_col = positions[b].reshape(S, 1)
        for l in range(depth):
            lam_init = 0.8 - 0.6 * math.exp(-0.3 * l)
            p = {
                "g_mix": g_mix[l].reshape(1, D),
                "w_in": w_in[l].astype(BF16),
                "w_va_t": w_in[l][:, 2 * GROUP_COLS:3 * GROUP_COLS].T.astype(BF16),
                "lam_vecs": jnp.stack([lambda_q1[l], lambda_k1[l], lambda_q2[l], lambda_k2[l]]).astype(F32),
                "g_subln": g_subln[l].reshape(DA_V, 1).astype(F32),
                "w_out": w_out[l].astype(BF16),
                "g_cross": g_cross[l].reshape(1, D),
                "g_mem": g_mem[l].reshape(1, D),
                "w_xq": w_xq[l].astype(BF16),
                "w_xkv": w_xkv[l].astype(BF16),
                "w_xo": w_xo[l].astype(BF16),
                "g_mlp": g_mlp[l].reshape(1, D),
                "w_up": w_up[l].astype(BF16),
                "w_down": w_down[l].astype(BF16),
            }
            h = _layer(h, pos_col, invf, mem[b], p, lam_init, gf, l == depth - 1)
        outs.append(h)
    return jnp.stack(outs)
```

```python
import functools
import math

import numpy as np
import jax
import jax.numpy as jnp
from jax import lax
from jax.experimental import pallas as pl
from jax.experimental.pallas import tpu as pltpu

F32 = jnp.float32
BF16 = jnp.bfloat16

HEAD_DIM = 64
DA_HEADS = 4
DA_V = 2 * HEAD_DIM
SB_HEADS = 8
GROUP_COLS = 512
ROT_DIM = HEAD_DIM // 4
ROPE_THETA = 500000.0
X_HEADS = 4
EPS = 1e-6
LANES = 128
NEG_BIG = -1e30

SB_ZERO_LOG2 = 151.0

VMEM_LIMIT = 56 * 1024 * 1024

ROW_TILE = 512
DIFF_TILE = 512
SB_TILE = 256
V_ROWS = DA_V + 16


def _rms(x, g):
    ms = jnp.mean(x * x, axis=-1, keepdims=True)
    return x * lax.rsqrt(ms + EPS) * g


def _const_spec(shape):
    nd = len(shape)
    return pl.BlockSpec(shape, lambda *_: (0,) * nd, pipeline_mode=pl.Buffered(1))


def _in_proj_kernel(x_ref, pos_ref, invf_ref, g_ref, w_ref, wvt_ref,
                    qt_ref, ka_ref, vt_ref, qs_ref, ks_ref, vs_ref):
    ts = x_ref.shape[0]
    xn = _rms(x_ref[...], g_ref[...]).astype(BF16)

    half = ROT_DIM // 2
    ang = invf_ref[...] * pos_ref[...].astype(F32)
    cos8 = jnp.cos(ang)
    sin8 = jnp.sin(ang)
    one8 = jnp.ones_like(cos8)
    zero8 = jnp.zeros_like(cos8)
    groups = range(LANES // half)
    in_x1 = [g % (HEAD_DIM // half) == 0 for g in groups]
    in_x2 = [g % (HEAD_DIM // half) == 1 for g in groups]
    cos = jnp.concatenate([cos8 if a or b else one8 for a, b in zip(in_x1, in_x2)], axis=0).T
    s_up = jnp.concatenate([-sin8 if a else zero8 for a in in_x1], axis=0).T
    s_dn = jnp.concatenate([sin8 if b else zero8 for b in in_x2], axis=0).T

    def group(idx):
        w = w_ref[:, idx * GROUP_COLS:(idx + 1) * GROUP_COLS]
        return jnp.dot(xn, w, preferred_element_type=F32)

    def rope(u, scale):
        parts = []
        for c in range(GROUP_COLS // LANES):
            uc = u[:, c * LANES:(c + 1) * LANES]
            r = (uc * cos + pltpu.roll(uc, LANES - ROT_DIM // 2, axis=1) * s_up
                 + pltpu.roll(uc, ROT_DIM // 2, axis=1) * s_dn)
            parts.append(r * scale)
        return jnp.concatenate(parts, axis=1)

    q_scale = math.log2(math.e) / math.sqrt(HEAD_DIM)

    qt = rope(group(0), q_scale).T
    zero_half = jnp.zeros((HEAD_DIM, ts), BF16)
    for hc in range(2 * DA_HEADS):
        c = hc % 2
        piece = qt[hc * HEAD_DIM:(hc + 1) * HEAD_DIM].astype(BF16)
        qt_ref[hc * LANES + c * HEAD_DIM:hc * LANES + (c + 1) * HEAD_DIM, :] = piece
        qt_ref[hc * LANES + (1 - c) * HEAD_DIM:hc * LANES + (2 - c) * HEAD_DIM, :] = zero_half

    ka_ref[...] = rope(group(1), 1.0).astype(BF16)

    vt = lax.dot_general(wvt_ref[...], xn, (((1,), (1,)), ((), ())),
                         preferred_element_type=F32)
    pad_row = lax.broadcasted_iota(jnp.int32, (V_ROWS - DA_V, ts), 0)
    ones_pad = jnp.where(pad_row == 0, 1.0, 0.0).astype(BF16)
    for h in range(DA_HEADS):
        vt_ref[0, h * V_ROWS:h * V_ROWS + DA_V, :] = vt[h * DA_V:(h + 1) * DA_V].astype(BF16)
        vt_ref[0, h * V_ROWS + DA_V:(h + 1) * V_ROWS, :] = ones_pad

    qs_ref[...] = (group(3) * q_scale).astype(BF16)
    ks_ref[...] = group(4).astype(BF16)
    vs_ref[...] = group(5).astype(BF16)


def _in_proj(x, pos_row, invf, g, w, wvt, ts):
    S, D = x.shape
    nt = S // ts
    flat = jax.ShapeDtypeStruct((S, GROUP_COLS), BF16)
    row = lambda i: (i, 0)
    flat_spec = pl.BlockSpec((ts, GROUP_COLS), row)
    return pl.pallas_call(
        _in_proj_kernel,
        grid=(nt,),
        in_specs=[
            pl.BlockSpec((ts, D), row),
            pl.BlockSpec((1, ts), lambda i: (0, i)),
            _const_spec(invf.shape),
            _const_spec((1, D)),
            _const_spec(w.shape),
            _const_spec(wvt.shape),
        ],
        out_specs=[
            pl.BlockSpec((2 * DA_HEADS * LANES, ts), lambda i: (0, i)),
            flat_spec,
            pl.BlockSpec((1, DA_HEADS * V_ROWS, ts), lambda i: (i, 0, 0)),
            flat_spec, flat_spec, flat_spec,
        ],
        out_shape=[
            jax.ShapeDtypeStruct((2 * DA_HEADS * LANES, S), BF16),
            flat,
            jax.ShapeDtypeStruct((nt, DA_HEADS * V_ROWS, ts), BF16),
            flat, flat, flat,
        ],
        compiler_params=pltpu.CompilerParams(
            dimension_semantics=("arbitrary",), vmem_limit_bytes=VMEM_LIMIT),
        name="in_proj",
    )(x, pos_row, invf, g, w, wvt)


def _diff_attn_kernel(lam_ref, g_ref, qt_ref, k_ref, vt_ref, o_ref,
                      acc_ref, m_ref, sa_ref, sb_ref, ma_ref, mb_ref, *, lam_init):
    i = pl.program_id(1)
    t = qt_ref.shape[1]
    acc_ref[...] = jnp.zeros_like(acc_ref)
    m_ref[...] = jnp.full_like(m_ref, NEG_BIG)

    def scores(j, s_ref, top_ref):
        kb = k_ref[pl.ds(pl.multiple_of(j * t, t), t), :]
        for c in range(2):
            st = jnp.dot(kb, qt_ref[c * LANES:(c + 1) * LANES, :], preferred_element_type=F32)
            s_ref[c] = st
            top_ref[c] = jnp.max(st, axis=0, keepdims=True)

    def accumulate(j, s_ref, top_ref, diagonal):
        vb = vt_ref[j]
        for c in range(2):
            st = s_ref[c]
            if diagonal:
                key = lax.broadcasted_iota(jnp.int32, st.shape, 0)
                qry = lax.broadcasted_iota(jnp.int32, st.shape, 1)
                st = jnp.where(key <= qry, st, NEG_BIG)
                top = jnp.max(st, axis=0, keepdims=True)
            else:
                top = top_ref[c]
            m_old = m_ref[c]
            m_new = jnp.maximum(m_old, top)
            alpha = jnp.exp2(m_old - m_new)
            p = jnp.exp2(st - m_new).astype(BF16)
            acc_ref[c] = alpha * acc_ref[c] + jnp.dot(vb, p, preferred_element_type=F32)
            m_ref[c] = m_new

    scores(0, sa_ref, ma_ref)

    def pair(n, carry):
        j = 2 * n
        scores(j + 1, sb_ref, mb_ref)
        accumulate(j, sa_ref, ma_ref, False)
        scores(j + 2, sa_ref, ma_ref)
        accumulate(j + 1, sb_ref, mb_ref, False)
        return carry

    lax.fori_loop(0, i // 2, pair, 0)

    @pl.when(i % 2 == 1)
    def _():
        scores(i, sb_ref, mb_ref)
        accumulate(i - 1, sa_ref, ma_ref, False)
        accumulate(i, sb_ref, mb_ref, True)

    @pl.when(i % 2 == 0)
    def _():
        accumulate(i, sa_ref, ma_ref, True)

    lam_v = lam_ref[...]
    lam = (jnp.exp(jnp.sum(lam_v[0:1] * lam_v[1:2], axis=1, keepdims=True))
           - jnp.exp(jnp.sum(lam_v[2:3] * lam_v[3:4], axis=1, keepdims=True)) + lam_init)
    a0 = acc_ref[0]
    a1 = acc_ref[1]
    o = a0[:DA_V] / a0[DA_V:DA_V + 1] - lam * (a1[:DA_V] / a1[DA_V:DA_V + 1])
    ms = jnp.mean(o * o, axis=0, keepdims=True)
    y = o * lax.rsqrt(ms + EPS) * g_ref[...] * (1.0 - lam_init)
    o_ref[...] = y.T.astype(o_ref.dtype)


def _diff_attn(lam_vecs, g_col, qt, ka, vtblk, lam_init):
    S = ka.shape[0]
    nk, _, t = vtblk.shape
    return pl.pallas_call(
        functools.partial(_diff_attn_kernel, lam_init=lam_init),
        grid=(DA_HEADS, nk),
        in_specs=[
            _const_spec(lam_vecs.shape),
            _const_spec(g_col.shape),
            pl.BlockSpec((2 * LANES, t), lambda h, i: (h, i)),
            pl.BlockSpec((S, LANES), lambda h, i: (0, h)),
            pl.BlockSpec((nk, V_ROWS, t), lambda h, i: (0, h, 0)),
        ],
        out_specs=pl.BlockSpec((t, DA_V), lambda h, i: (i, h)),
        out_shape=jax.ShapeDtypeStruct((S, DA_HEADS * DA_V), BF16),
        scratch_shapes=[pltpu.VMEM((2, V_ROWS, t), F32), pltpu.VMEM((2, 1, t), F32),
                        pltpu.VMEM((2, t, t), F32), pltpu.VMEM((2, t, t), F32),
                        pltpu.VMEM((2, 1, t), F32), pltpu.VMEM((2, 1, t), F32)],
        compiler_params=pltpu.CompilerParams(
            dimension_semantics=("arbitrary", "arbitrary"), vmem_limit_bytes=VMEM_LIMIT),
        name="diff_attn",
    )(lam_vecs, g_col, qt, ka, vtblk)


def _sb_attn_kernel(q_ref, k_ref, v_ref, o_ref, acc_ref, carry_ref):
    i = pl.program_id(0)
    t = q_ref.shape[0]
    pairs = q_ref.shape[1] // LANES
    lane = lax.broadcasted_iota(jnp.int32, (t, LANES), 1)
    first = lane < HEAD_DIM
    row = lax.broadcasted_iota(jnp.int32, (t, t), 0)
    col = lax.broadcasted_iota(jnp.int32, (t, t), 1)
    tri = jnp.where(row > col, 1.0, 0.0).astype(BF16)
    strict = jnp.concatenate([col < row, col < row], axis=0)

    def block(j, p, diagonal):
        lanes = slice(p * LANES, (p + 1) * LANES)
        q2 = q_ref[:, lanes]
        zeros = jnp.zeros_like(q2)
        qs = jnp.concatenate([jnp.where(first, q2, zeros), jnp.where(first, zeros, q2)], axis=0)
        rows = pl.ds(pl.multiple_of(j * t, t), t)
        kb = k_ref[rows, lanes]
        vb = v_ref[rows, lanes]
        z = lax.dot_general(qs, kb, (((1,), (1,)), ((), ())),
                            preferred_element_type=F32)
        sp = jnp.maximum(z, 0.0) + jnp.log2(1.0 + jnp.exp2(-jnp.abs(z)))
        if diagonal:
            sp = jnp.where(strict, sp, 0.0)
            carry = jnp.zeros((2 * t, 1), F32)
        else:
            carry = carry_ref[p]
        hi = sp.astype(BF16)
        between = jnp.dot(hi, tri, preferred_element_type=F32) + carry
        a = jnp.exp2(z - sp - between)
        if diagonal:
            a = jnp.where(strict, a, 0.0)
        av = jnp.dot(a.astype(BF16), vb, preferred_element_type=F32)
        acc_ref[p] = av if diagonal else acc_ref[p] + av
        carry = carry + jnp.sum(sp, axis=1, keepdims=True)
        carry_ref[p] = carry
        return carry

    def all_heads(j, diagonal):
        low = block(j, 0, diagonal)
        for p in range(1, pairs):
            low = jnp.minimum(low, block(j, p, diagonal))
        return jnp.min(low)

    def cond(state):
        return jnp.logical_and(state[0] >= 0, state[1] < SB_ZERO_LOG2)

    def body(state):
        j = state[0]
        return j - 1, all_heads(j, False)

    lax.while_loop(cond, body, (i - 1, all_heads(i, True)))
    for p in range(pairs):
        acc = acc_ref[p]
        o_ref[:, p * LANES:(p + 1) * LANES] = jnp.where(first, acc[:t], acc[t:]).astype(o_ref.dtype)


def _sb_attn(q, k, v, t):
    S, cols = q.shape
    pairs = cols // LANES
    return pl.pallas_call(
        _sb_attn_kernel,
        grid=(S // t,),
        in_specs=[
            pl.BlockSpec((t, cols), lambda i: (i, 0)),
            _const_spec(k.shape),
            _const_spec(v.shape),
        ],
        out_specs=pl.BlockSpec((t, cols), lambda i: (i, 0)),
        out_shape=jax.ShapeDtypeStruct((S, cols), BF16),
        scratch_shapes=[pltpu.VMEM((pairs, 2 * t, LANES), F32), pltpu.VMEM((pairs, 2 * t, 1), F32)],
        compiler_params=pltpu.CompilerParams(
            dimension_semantics=("arbitrary",), vmem_limit_bytes=VMEM_LIMIT),
        name="sb_attn",
    )(q, k, v)


def _mem_kv_kernel(mem_ref, g_ref, w_ref, kv_ref):
    mn = _rms(mem_ref[...], g_ref[...]).astype(BF16)
    kv_ref[...] = jnp.dot(mn, w_ref[...], preferred_element_type=F32).astype(kv_ref.dtype)


def _mem_kv(mem, g, w):
    M = mem.shape[0]
    return pl.pallas_call(
        _mem_kv_kernel,
        out_shape=jax.ShapeDtypeStruct((M, w.shape[1]), BF16),
        compiler_params=pltpu.CompilerParams(vmem_limit_bytes=VMEM_LIMIT),
        name="mem_kv",
    )(mem, g, w)


def _post_mlp_kernel(x_ref, ya_ref, yb_ref, wout_ref, gc_ref, wxq_ref, kv_ref, wxo_ref,
                     gm_ref, wup_ref, wdown_ref, gf_ref, o_ref, *, final_norm):
    D = x_ref.shape[1]
    half = ya_ref.shape[1]
    h = (x_ref[...]
         + jnp.dot(ya_ref[...], wout_ref[:half], preferred_element_type=F32)
         + jnp.dot(yb_ref[...], wout_ref[half:], preferred_element_type=F32))

    hn = _rms(h, gc_ref[...]).astype(BF16)
    xd = D // X_HEADS
    q = jnp.dot(hn, wxq_ref[...], preferred_element_type=F32) * (1.0 / math.sqrt(xd))
    q = q.astype(BF16)
    upd = jnp.zeros_like(h)
    for hd in range(X_HEADS):
        qh = q[:, hd * xd:(hd + 1) * xd]
        kh = kv_ref[:, hd * xd:(hd + 1) * xd]
        vh = kv_ref[:, D + hd * xd:D + (hd + 1) * xd]
        s = lax.dot_general(qh, kh, (((1,), (1,)), ((), ())), preferred_element_type=F32)
        p = jnp.exp(s - jnp.max(s, axis=-1, keepdims=True))
        l = jnp.sum(p, axis=-1, keepdims=True)
        oh = jnp.dot(p.astype(BF16), vh, preferred_element_type=F32) / l
        upd = upd + jnp.dot(oh.astype(BF16), wxo_ref[hd * xd:(hd + 1) * xd, :],
                            preferred_element_type=F32)
    h = h + upd

    hn = _rms(h, gm_ref[...]).astype(BF16)
    upd = jnp.zeros_like(h)
    for c in range(wup_ref.shape[1] // D):
        a = jnp.dot(hn, wup_ref[:, c * D:(c + 1) * D], preferred_element_type=F32)
        a = jnp.square(jnp.maximum(a, 0.0)).astype(BF16)
        upd = upd + jnp.dot(a, wdown_ref[c * D:(c + 1) * D, :], preferred_element_type=F32)
    h = h + upd
    if final_norm:
        h = _rms(h, gf_ref[...])
    o_ref[...] = h


def _post_mlp(x, ya, yb, kv, p, gf, final_norm):
    S, D = x.shape
    ts = min(ROW_TILE, S)
    row = lambda i: (i, 0)
    consts = [p["w_out"], p["g_cross"], p["w_xq"], kv, p["w_xo"],
              p["g_mlp"], p["w_up"], p["w_down"], gf]
    return pl.pallas_call(
        functools.partial(_post_mlp_kernel, final_norm=final_norm),
        grid=(S // ts,),
        in_specs=[pl.BlockSpec((ts, D), row),
                  pl.BlockSpec((ts, ya.shape[1]), row),
                  pl.BlockSpec((ts, yb.shape[1]), row)] + [_const_spec(c.shape) for c in consts],
        out_specs=pl.BlockSpec((ts, D), row),
        out_shape=jax.ShapeDtypeStruct((S, D), F32),
        compiler_params=pltpu.CompilerParams(
            dimension_semantics=("arbitrary",), vmem_limit_bytes=VMEM_LIMIT),
        name="post_mlp",
    )(x, ya, yb, *consts)


def _rope_inv_freq():
    inv = np.float32(ROPE_THETA) ** (-np.arange(0, ROT_DIM, 2, dtype=np.float32) / np.float32(ROT_DIM))
    return jnp.asarray(inv.reshape(ROT_DIM // 2, 1))


def _layer(h, pos_row, invf, mem, p, lam_init, gf, final_norm):
    S, D = h.shape
    td = min(DIFF_TILE, S)
    qt, ka, vtblk, qs, ks, vs = _in_proj(h, pos_row, invf, p["g_mix"], p["w_in"], p["w_va_t"], td)
    ya = _diff_attn(p["lam_vecs"], p["g_subln"], qt, ka, vtblk, lam_init)
    yb = _sb_attn(qs, ks, vs, min(SB_TILE, S))
    kv = _mem_kv(mem, p["g_mem"], p["w_xkv"])
    return _post_mlp(h, ya, yb, kv, p, gf, final_norm)


def kernel(x, mem, positions, g_mix, w_in, lambda_q1, lambda_k1, lambda_q2, lambda_k2, g_subln, w_out, g_cross, g_mem, w_xq, w_xkv, w_xo, g_mlp, w_up, w_down, g_final):
    B, S, D = x.shape
    depth = w_in.shape[0]
    invf = _rope_inv_freq()
    gf = g_final.reshape(1, D).astype(F32)
    outs = []
    for b in range(B):
        h = x[b]
        pos_row = positions[b].reshape(1, S)
        for l in range(depth):
            lam_init = 0.8 - 0.6 * math.exp(-0.3 * l)
            p = {
                "g_mix": g_mix[l].reshape(1, D),
                "w_in": w_in[l].astype(BF16),
                "w_va_t": w_in[l][:, 2 * GROUP_COLS:3 * GROUP_COLS].T.astype(BF16),
                "lam_vecs": jnp.stack([lambda_q1[l], lambda_k1[l], lambda_q2[l], lambda_k2[l]]).astype(F32),
                "g_subln": g_subln[l].reshape(DA_V, 1).astype(F32),
                "w_out": w_out[l].astype(BF16),
                "g_cross": g_cross[l].reshape(1, D),
                "g_mem": g_mem[l].reshape(1, D),
                "w_xq": w_xq[l].astype(BF16),
                "w_xkv": w_xkv[l].astype(BF16),
                "w_xo": w_xo[l].astype(BF16),
                "g_mlp": g_mlp[l].reshape(1, D),
                "w_up": w_up[l].astype(BF16),
                "w_down": w_down[l].astype(BF16),
            }
            h = _layer(h, pos_row, invf, mem[b], p, lam_init, gf, l == depth - 1)
        outs.append(h)
    return jnp.stack(outs)
```

```python
import functools
import math

import numpy as np
import jax
import jax.numpy as jnp
from jax import lax
from jax.experimental import pallas as pl
from jax.experimental.pallas import tpu as pltpu

F32 = jnp.float32
BF16 = jnp.bfloat16

HEAD_DIM = 64
DA_HEADS = 4
DA_V = 2 * HEAD_DIM
SB_HEADS = 8
GROUP_COLS = 512
ROT_DIM = HEAD_DIM // 4
ROPE_THETA = 500000.0
X_HEADS = 4
EPS = 1e-6
LANES = 128
NEG_BIG = -1e30

SB_ZERO_LOG2 = 151.0

LAGGED_MAX_SCORE = 45.0
KEY_NORM_SLACK = 1.02

VMEM_LIMIT = 56 * 1024 * 1024

ROW_TILE = 512
DIFF_TILE = 512
SB_TILE = 256
V_ROWS = DA_V + 16


def _rms(x, g):
    ms = jnp.mean(x * x, axis=-1, keepdims=True)
    return x * lax.rsqrt(ms + EPS) * g


def _const_spec(shape):
    nd = len(shape)
    return pl.BlockSpec(shape, lambda *_: (0,) * nd, pipeline_mode=pl.Buffered(1))


def _in_proj_kernel(x_ref, pos_ref, invf_ref, g_ref, w_ref, wvt_ref,
                    qt_ref, ka_ref, vt_ref, qs_ref, ks_ref, vs_ref, kn_ref):
    ts = x_ref.shape[0]
    xn = _rms(x_ref[...], g_ref[...]).astype(BF16)

    half = ROT_DIM // 2
    ang = invf_ref[...] * pos_ref[...].astype(F32)
    cos8 = jnp.cos(ang)
    sin8 = jnp.sin(ang)
    one8 = jnp.ones_like(cos8)
    zero8 = jnp.zeros_like(cos8)
    groups = range(LANES // half)
    in_x1 = [g % (HEAD_DIM // half) == 0 for g in groups]
    in_x2 = [g % (HEAD_DIM // half) == 1 for g in groups]
    cos = jnp.concatenate([cos8 if a or b else one8 for a, b in zip(in_x1, in_x2)], axis=0).T
    s_up = jnp.concatenate([-sin8 if a else zero8 for a in in_x1], axis=0).T
    s_dn = jnp.concatenate([sin8 if b else zero8 for b in in_x2], axis=0).T

    def group(idx):
        w = w_ref[:, idx * GROUP_COLS:(idx + 1) * GROUP_COLS]
        return jnp.dot(xn, w, preferred_element_type=F32)

    def rope(u, scale):
        parts = []
        for c in range(GROUP_COLS // LANES):
            uc = u[:, c * LANES:(c + 1) * LANES]
            r = (uc * cos + pltpu.roll(uc, LANES - ROT_DIM // 2, axis=1) * s_up
                 + pltpu.roll(uc, ROT_DIM // 2, axis=1) * s_dn)
            parts.append(r * scale)
        return jnp.concatenate(parts, axis=1)

    q_scale = math.log2(math.e) / math.sqrt(HEAD_DIM)

    qt = rope(group(0), q_scale).T
    zero_half = jnp.zeros((HEAD_DIM, ts), BF16)
    for hc in range(2 * DA_HEADS):
        c = hc % 2
        piece = qt[hc * HEAD_DIM:(hc + 1) * HEAD_DIM].astype(BF16)
        qt_ref[hc * LANES + c * HEAD_DIM:hc * LANES + (c + 1) * HEAD_DIM, :] = piece
        qt_ref[hc * LANES + (1 - c) * HEAD_DIM:hc * LANES + (2 - c) * HEAD_DIM, :] = zero_half

    ka = rope(group(1), 1.0)
    ka_ref[...] = ka.astype(BF16)
    for h in range(DA_HEADS):
        kh = ka[:, h * LANES:(h + 1) * LANES]
        norm2 = jnp.max(jnp.sum(kh * kh, axis=1, keepdims=True), axis=0, keepdims=True)
        kn_ref[0, :, h * LANES:(h + 1) * LANES] = jnp.broadcast_to(norm2, (1, LANES))

    vt = lax.dot_general(wvt_ref[...], xn, (((1,), (1,)), ((), ())),
                         preferred_element_type=F32)
    pad_row = lax.broadcasted_iota(jnp.int32, (V_ROWS - DA_V, ts), 0)
    ones_pad = jnp.where(pad_row == 0, 1.0, 0.0).astype(BF16)
    for h in range(DA_HEADS):
        vt_ref[0, h * V_ROWS:h * V_ROWS + DA_V, :] = vt[h * DA_V:(h + 1) * DA_V].astype(BF16)
        vt_ref[0, h * V_ROWS + DA_V:(h + 1) * V_ROWS, :] = ones_pad

    qs_ref[...] = (group(3) * q_scale).astype(BF16)
    ks_ref[...] = group(4).astype(BF16)
    vs_ref[...] = group(5).astype(BF16)


def _in_proj(x, pos_row, invf, g, w, wvt, ts):
    S, D = x.shape
    nt = S // ts
    flat = jax.ShapeDtypeStruct((S, GROUP_COLS), BF16)
    row = lambda i: (i, 0)
    flat_spec = pl.BlockSpec((ts, GROUP_COLS), row)
    return pl.pallas_call(
        _in_proj_kernel,
        grid=(nt,),
        in_specs=[
            pl.BlockSpec((ts, D), row),
            pl.BlockSpec((1, ts), lambda i: (0, i)),
            _const_spec(invf.shape),
            _const_spec((1, D)),
            _const_spec(w.shape),
            _const_spec(wvt.shape),
        ],
        out_specs=[
            pl.BlockSpec((2 * DA_HEADS * LANES, ts), lambda i: (0, i)),
            flat_spec,
            pl.BlockSpec((1, DA_HEADS * V_ROWS, ts), lambda i: (i, 0, 0)),
            flat_spec, flat_spec, flat_spec,
            pl.BlockSpec((1, 1, DA_HEADS * LANES), lambda i: (i, 0, 0)),
        ],
        out_shape=[
            jax.ShapeDtypeStruct((2 * DA_HEADS * LANES, S), BF16),
            flat,
            jax.ShapeDtypeStruct((nt, DA_HEADS * V_ROWS, ts), BF16),
            flat, flat, flat,
            jax.ShapeDtypeStruct((nt, 1, DA_HEADS * LANES), F32),
        ],
        compiler_params=pltpu.CompilerParams(
            dimension_semantics=("arbitrary",), vmem_limit_bytes=VMEM_LIMIT),
        name="in_proj",
    )(x, pos_row, invf, g, w, wvt)


def _diff_attn_kernel(lam_ref, g_ref, kn_ref, qt_ref, k_ref, vt_ref, o_ref,
                      acc_ref, m_ref, pa_ref, pb_ref, ma_ref, mb_ref, fa_ref, fb_ref, *, lam_init):
    i = pl.program_id(1)
    t = qt_ref.shape[1]
    acc_ref[...] = jnp.zeros_like(acc_ref)

    def scores(j, c, visible_offset):
        kb = k_ref[pl.ds(pl.multiple_of(j * t, t), t), :]
        st = jnp.dot(kb, qt_ref[c * LANES:(c + 1) * LANES, :], preferred_element_type=F32)
        if visible_offset is not None:
            key = lax.broadcasted_iota(jnp.int32, st.shape, 0)
            qry = lax.broadcasted_iota(jnp.int32, st.shape, 1)
            st = jnp.where(key - qry <= visible_offset, st, NEG_BIG)
        return st

    def exact_block(j, diagonal):
        vb = vt_ref[j]
        for c in range(2):
            st = scores(j, c, 0 if diagonal else None)
            m_old = m_ref[c]
            m_new = jnp.maximum(m_old, jnp.max(st, axis=0, keepdims=True))
            alpha = jnp.exp2(m_old - m_new)
            p = jnp.exp2(st - m_new).astype(BF16)
            acc_ref[c] = alpha * acc_ref[c] + jnp.dot(vb, p, preferred_element_type=F32)
            m_ref[c] = m_new

    def lagged_weights(j, p_ref, m_in_ref, m_out_ref, f_ref, visible_offset=None):
        for c in range(2):
            st = scores(j, c, visible_offset)
            m_old = m_in_ref[c]
            p_ref[c] = jnp.exp2(st - m_old).astype(BF16)
            m_new = jnp.maximum(m_old, jnp.max(st, axis=0, keepdims=True))
            f_ref[c] = jnp.exp2(m_old - m_new)
            m_out_ref[c] = m_new

    def lagged_accumulate(j, p_ref, f_ref):
        vb = vt_ref[j]
        for c in range(2):
            acc_ref[c] = f_ref[c] * (acc_ref[c] + jnp.dot(vb, p_ref[c], preferred_element_type=F32))

    tile = lax.broadcasted_iota(jnp.int32, kn_ref.shape, 0)
    kmax2 = jnp.max(jnp.where(tile <= i, kn_ref[...], 0.0)) * KEY_NORM_SLACK
    qn2 = jnp.zeros((1, t), F32)
    for c in range(2):
        qc = qt_ref[c * LANES:(c + 1) * LANES, :].astype(F32)
        qn2 = jnp.maximum(qn2, jnp.sum(qc * qc, axis=0, keepdims=True))
    safe = jnp.max(qn2) * kmax2 <= LAGGED_MAX_SCORE * LAGGED_MAX_SCORE

    @pl.when(safe)
    def _():
        mb_ref[...] = jnp.zeros_like(mb_ref)
        lagged_weights(0, pa_ref, mb_ref, ma_ref, fa_ref, visible_offset=i * t)

        def pair(n, carry):
            j = 2 * n
            lagged_weights(j + 1, pb_ref, ma_ref, mb_ref, fb_ref)
            lagged_accumulate(j, pa_ref, fa_ref)
            lagged_weights(j + 2, pa_ref, mb_ref, ma_ref, fa_ref, visible_offset=(i - j - 2) * t)
            lagged_accumulate(j + 1, pb_ref, fb_ref)
            return carry

        lax.fori_loop(0, i // 2, pair, 0)

        @pl.when(i % 2 == 1)
        def _():
            lagged_weights(i, pb_ref, ma_ref, mb_ref, fb_ref, visible_offset=0)
            lagged_accumulate(i - 1, pa_ref, fa_ref)
            lagged_accumulate(i, pb_ref, fb_ref)

        @pl.when(i % 2 == 0)
        def _():
            lagged_accumulate(i, pa_ref, fa_ref)

    @pl.when(jnp.logical_not(safe))
    def _():
        m_ref[...] = jnp.full_like(m_ref, NEG_BIG)

        def one(j, carry):
            exact_block(j, False)
            return carry

        lax.fori_loop(0, i, one, 0)
        exact_block(i, True)

    lam_v = lam_ref[...]
    lam = (jnp.exp(jnp.sum(lam_v[0:1] * lam_v[1:2], axis=1, keepdims=True))
           - jnp.exp(jnp.sum(lam_v[2:3] * lam_v[3:4], axis=1, keepdims=True)) + lam_init)
    a0 = acc_ref[0]
    a1 = acc_ref[1]
    o = a0[:DA_V] / a0[DA_V:DA_V + 1] - lam * (a1[:DA_V] / a1[DA_V:DA_V + 1])
    ms = jnp.mean(o * o, axis=0, keepdims=True)
    y = o * lax.rsqrt(ms + EPS) * g_ref[...] * (1.0 - lam_init)
    o_ref[...] = y.T.astype(o_ref.dtype)


def _diff_attn(lam_vecs, g_col, kn, qt, ka, vtblk, lam_init):
    S = ka.shape[0]
    nk, _, t = vtblk.shape
    return pl.pallas_call(
        functools.partial(_diff_attn_kernel, lam_init=lam_init),
        grid=(DA_HEADS, nk),
        in_specs=[
            _const_spec(lam_vecs.shape),
            _const_spec(g_col.shape),
            pl.BlockSpec((nk, 1, LANES), lambda h, i: (0, 0, h)),
            pl.BlockSpec((2 * LANES, t), lambda h, i: (h, i)),
            pl.BlockSpec((S, LANES), lambda h, i: (0, h)),
            pl.BlockSpec((nk, V_ROWS, t), lambda h, i: (0, h, 0)),
        ],
        out_specs=pl.BlockSpec((t, DA_V), lambda h, i: (i, h)),
        out_shape=jax.ShapeDtypeStruct((S, DA_HEADS * DA_V), BF16),
        scratch_shapes=[pltpu.VMEM((2, V_ROWS, t), F32), pltpu.VMEM((2, 1, t), F32),
                        pltpu.VMEM((2, t, t), BF16), pltpu.VMEM((2, t, t), BF16),
                        pltpu.VMEM((2, 1, t), F32), pltpu.VMEM((2, 1, t), F32),
                        pltpu.VMEM((2, 1, t), F32), pltpu.VMEM((2, 1, t), F32)],
        compiler_params=pltpu.CompilerParams(
            dimension_semantics=("arbitrary", "arbitrary"), vmem_limit_bytes=VMEM_LIMIT),
        name="diff_attn",
    )(lam_vecs, g_col, kn, qt, ka, vtblk)


def _sb_attn_kernel(q_ref, k_ref, v_ref, o_ref, acc_ref, carry_ref):
    i = pl.program_id(0)
    t = q_ref.shape[0]
    pairs = q_ref.shape[1] // LANES
    lane = lax.broadcasted_iota(jnp.int32, (t, LANES), 1)
    first = lane < HEAD_DIM
    row = lax.broadcasted_iota(jnp.int32, (t, t), 0)
    col = lax.broadcasted_iota(jnp.int32, (t, t), 1)
    tri = jnp.where(row > col, 1.0, 0.0).astype(BF16)
    strict = jnp.concatenate([col < row, col < row], axis=0)

    def block(j, p, diagonal):
        lanes = slice(p * LANES, (p + 1) * LANES)
        q2 = q_ref[:, lanes]
        zeros = jnp.zeros_like(q2)
        qs = jnp.concatenate([jnp.where(first, q2, zeros), jnp.where(first, zeros, q2)], axis=0)
        rows = pl.ds(pl.multiple_of(j * t, t), t)
        kb = k_ref[rows, lanes]
        vb = v_ref[rows, lanes]
        z = lax.dot_general(qs, kb, (((1,), (1,)), ((), ())),
                            preferred_element_type=F32)
        sp = jnp.maximum(z, 0.0) + jnp.log2(1.0 + jnp.exp2(-jnp.abs(z)))
        if diagonal:
            sp = jnp.where(strict, sp, 0.0)
            carry = jnp.zeros((2 * t, 1), F32)
        else:
            carry = carry_ref[p]
        hi = sp.astype(BF16)
        between = jnp.dot(hi, tri, preferred_element_type=F32) + carry
        a = jnp.exp2(z - sp - between)
        if diagonal:
            a = jnp.where(strict, a, 0.0)
        av = jnp.dot(a.astype(BF16), vb, preferred_element_type=F32)
        acc_ref[p] = av if diagonal else acc_ref[p] + av
        carry = carry + jnp.sum(sp, axis=1, keepdims=True)
        carry_ref[p] = carry
        return carry

    def all_heads(j, diagonal):
        low = block(j, 0, diagonal)
        for p in range(1, pairs):
            low = jnp.minimum(low, block(j, p, diagonal))
        return jnp.min(low)

    def cond(state):
        return jnp.logical_and(state[0] >= 0, state[1] < SB_ZERO_LOG2)

    def body(state):
        j = state[0]
        return j - 1, all_heads(j, False)

    lax.while_loop(cond, body, (i - 1, all_heads(i, True)))
    for p in range(pairs):
        acc = acc_ref[p]
        o_ref[:, p * LANES:(p + 1) * LANES] = jnp.where(first, acc[:t], acc[t:]).astype(o_ref.dtype)


def _sb_attn(q, k, v, t):
    S, cols = q.shape
    pairs = cols // LANES
    return pl.pallas_call(
        _sb_attn_kernel,
        grid=(S // t,),
        in_specs=[
            pl.BlockSpec((t, cols), lambda i: (i, 0)),
            _const_spec(k.shape),
            _const_spec(v.shape),
        ],
        out_specs=pl.BlockSpec((t, cols), lambda i: (i, 0)),
        out_shape=jax.ShapeDtypeStruct((S, cols), BF16),
        scratch_shapes=[pltpu.VMEM((pairs, 2 * t, LANES), F32), pltpu.VMEM((pairs, 2 * t, 1), F32)],
        compiler_params=pltpu.CompilerParams(
            dimension_semantics=("arbitrary",), vmem_limit_bytes=VMEM_LIMIT),
        name="sb_attn",
    )(q, k, v)


def _mem_kv_kernel(mem_ref, g_ref, w_ref, kv_ref):
    mn = _rms(mem_ref[...], g_ref[...]).astype(BF16)
    kv_ref[...] = jnp.dot(mn, w_ref[...], preferred_element_type=F32).astype(kv_ref.dtype)


def _mem_kv(mem, g, w):
    M = mem.shape[0]
    return pl.pallas_call(
        _mem_kv_kernel,
        out_shape=jax.ShapeDtypeStruct((M, w.shape[1]), BF16),
        compiler_params=pltpu.CompilerParams(vmem_limit_bytes=VMEM_LIMIT),
        name="mem_kv",
    )(mem, g, w)


def _post_mlp_kernel(x_ref, ya_ref, yb_ref, wout_ref, gc_ref, wxq_ref, kv_ref, wxo_ref,
                     gm_ref, wup_ref, wdown_ref, gf_ref, o_ref, *, final_norm):
    D = x_ref.shape[1]
    half = ya_ref.shape[1]
    h = (x_ref[...]
         + jnp.dot(ya_ref[...], wout_ref[:half], preferred_element_type=F32)
         + jnp.dot(yb_ref[...], wout_ref[half:], preferred_element_type=F32))

    hn = _rms(h, gc_ref[...]).astype(BF16)
    xd = D // X_HEADS
    q = jnp.dot(hn, wxq_ref[...], preferred_element_type=F32) * (1.0 / math.sqrt(xd))
    q = q.astype(BF16)
    upd = jnp.zeros_like(h)
    for hd in range(X_HEADS):
        qh = q[:, hd * xd:(hd + 1) * xd]
        kh = kv_ref[:, hd * xd:(hd + 1) * xd]
        vh = kv_ref[:, D + hd * xd:D + (hd + 1) * xd]
        s = lax.dot_general(qh, kh, (((1,), (1,)), ((), ())), preferred_element_type=F32)
        p = jnp.exp(s - jnp.max(s, axis=-1, keepdims=True))
        l = jnp.sum(p, axis=-1, keepdims=True)
        oh = jnp.dot(p.astype(BF16), vh, preferred_element_type=F32) / l
        upd = upd + jnp.dot(oh.astype(BF16), wxo_ref[hd * xd:(hd + 1) * xd, :],
                            preferred_element_type=F32)
    h = h + upd

    hn = _rms(h, gm_ref[...]).astype(BF16)
    upd = jnp.zeros_like(h)
    for c in range(wup_ref.shape[1] // D):
        a = jnp.dot(hn, wup_ref[:, c * D:(c + 1) * D], preferred_element_type=F32)
        a = jnp.square(jnp.maximum(a, 0.0)).astype(BF16)
        upd = upd + jnp.dot(a, wdown_ref[c * D:(c + 1) * D, :], preferred_element_type=F32)
    h = h + upd
    if final_norm:
        h = _rms(h, gf_ref[...])
    o_ref[...] = h


def _post_mlp(x, ya, yb, kv, p, gf, final_norm):
    S, D = x.shape
    ts = min(ROW_TILE, S)
    row = lambda i: (i, 0)
    consts = [p["w_out"], p["g_cross"], p["w_xq"], kv, p["w_xo"],
              p["g_mlp"], p["w_up"], p["w_down"], gf]
    return pl.pallas_call(
        functools.partial(_post_mlp_kernel, final_norm=final_norm),
        grid=(S // ts,),
        in_specs=[pl.BlockSpec((ts, D), row),
                  pl.BlockSpec((ts, ya.shape[1]), row),
                  pl.BlockSpec((ts, yb.shape[1]), row)] + [_const_spec(c.shape) for c in consts],
        out_specs=pl.BlockSpec((ts, D), row),
        out_shape=jax.ShapeDtypeStruct((S, D), F32),
        compiler_params=pltpu.CompilerParams(
            dimension_semantics=("arbitrary",), vmem_limit_bytes=VMEM_LIMIT),
        name="post_mlp",
    )(x, ya, yb, *consts)


def _rope_inv_freq():
    inv = np.float32(ROPE_THETA) ** (-np.arange(0, ROT_DIM, 2, dtype=np.float32) / np.float32(ROT_DIM))
    return jnp.asarray(inv.reshape(ROT_DIM // 2, 1))


def _layer(h, pos_row, invf, mem, p, lam_init, gf, final_norm):
    S, D = h.shape
    td = min(DIFF_TILE, S)
    qt, ka, vtblk, qs, ks, vs, kn = _in_proj(h, pos_row, invf, p["g_mix"], p["w_in"], p["w_va_t"], td)
    ya = _diff_attn(p["lam_vecs"], p["g_subln"], kn, qt, ka, vtblk, lam_init)
    yb = _sb_attn(qs, ks, vs, min(SB_TILE, S))
    kv = _mem_kv(mem, p["g_mem"], p["w_xkv"])
    return _post_mlp(h, ya, yb, kv, p, gf, final_norm)


def kernel(x, mem, positions, g_mix, w_in, lambda_q1, lambda_k1, lambda_q2, lambda_k2, g_subln, w_out, g_cross, g_mem, w_xq, w_xkv, w_xo, g_mlp, w_up, w_down, g_final):
    B, S, D = x.shape
    depth = w_in.shape[0]
    invf = _rope_inv_freq()
    gf = g_final.reshape(1, D).astype(F32)
    outs = []
    for b in range(B):
        h = x[b]
        pos_row = positions[b].reshape(1, S)
        for l in range(depth):
            lam_init = 0.8 - 0.6 * math.exp(-0.3 * l)
            p = {
                "g_mix": g_mix[l].reshape(1, D),
                "w_in": w_in[l].astype(BF16),
                "w_va_t": w_in[l][:, 2 * GROUP_COLS:3 * GROUP_COLS].T.astype(BF16),
                "lam_vecs": jnp.stack([lambda_q1[l], lambda_k1[l], lambda_q2[l], lambda_k2[l]]).astype(F32),
                "g_subln": g_subln[l].reshape(DA_V, 1).astype(F32),
                "w_out": w_out[l].astype(BF16),
                "g_cross": g_cross[l].reshape(1, D),
                "g_mem": g_mem[l].reshape(1, D),
                "w_xq": w_xq[l].astype(BF16),
                "w_xkv": w_xkv[l].astype(BF16),
                "w_xo": w_xo[l].astype(BF16),
                "g_mlp": g_mlp[l].reshape(1, D),
                "w_up": w_up[l].astype(BF16),
                "w_down": w_down[l].astype(BF16),
            }
            h = _layer(h, pos_row, invf, mem[b], p, lam_init, gf, l == depth - 1)
        outs.append(h)
    return jnp.stack(outs)
```

```python
import functools
import math

import numpy as np
import jax
import jax.numpy as jnp
from jax import lax
from jax.experimental import pallas as pl
from jax.experimental.pallas import tpu as pltpu

F32 = jnp.float32
BF16 = jnp.bfloat16

HEAD_DIM = 64
DA_HEADS = 4
DA_V = 2 * HEAD_DIM
SB_HEADS = 8
GROUP_COLS = 512
ROT_DIM = HEAD_DIM // 4
ROPE_THETA = 500000.0
X_HEADS = 4
EPS = 1e-6
LANES = 128
NEG_BIG = -1e30

SB_ZERO_LOG2 = 151.0

LAGGED_MAX_SCORE = 45.0
KEY_NORM_SLACK = 1.02

VMEM_LIMIT = 56 * 1024 * 1024

ROW_TILE = 512
DIFF_TILE = 512
DIFF_HEADS_PER_STEP = 4
SB_TILE = 256
V_ROWS = DA_V + 16


def _rms(x, g):
    ms = jnp.mean(x * x, axis=-1, keepdims=True)
    return x * lax.rsqrt(ms + EPS) * g


def _const_spec(shape):
    nd = len(shape)
    return pl.BlockSpec(shape, lambda *_: (0,) * nd, pipeline_mode=pl.Buffered(1))


def _in_proj_kernel(x_ref, pos_ref, invf_ref, g_ref, w_ref, wvt_ref,
                    qt_ref, ka_ref, vt_ref, qs_ref, ks_ref, vs_ref, kn_ref):
    ts = x_ref.shape[0]
    xn = _rms(x_ref[...], g_ref[...]).astype(BF16)

    half = ROT_DIM // 2
    ang = invf_ref[...] * pos_ref[...].astype(F32)
    cos8 = jnp.cos(ang)
    sin8 = jnp.sin(ang)
    one8 = jnp.ones_like(cos8)
    zero8 = jnp.zeros_like(cos8)
    groups = range(LANES // half)
    in_x1 = [g % (HEAD_DIM // half) == 0 for g in groups]
    in_x2 = [g % (HEAD_DIM // half) == 1 for g in groups]
    cos = jnp.concatenate([cos8 if a or b else one8 for a, b in zip(in_x1, in_x2)], axis=0).T
    s_up = jnp.concatenate([-sin8 if a else zero8 for a in in_x1], axis=0).T
    s_dn = jnp.concatenate([sin8 if b else zero8 for b in in_x2], axis=0).T

    def group(idx):
        w = w_ref[:, idx * GROUP_COLS:(idx + 1) * GROUP_COLS]
        return jnp.dot(xn, w, preferred_element_type=F32)

    def rope(u, scale):
        parts = []
        for c in range(GROUP_COLS // LANES):
            uc = u[:, c * LANES:(c + 1) * LANES]
            r = (uc * cos + pltpu.roll(uc, LANES - ROT_DIM // 2, axis=1) * s_up
                 + pltpu.roll(uc, ROT_DIM // 2, axis=1) * s_dn)
            parts.append(r * scale)
        return jnp.concatenate(parts, axis=1)

    q_scale = math.log2(math.e) / math.sqrt(HEAD_DIM)

    qt = rope(group(0), q_scale).T
    zero_half = jnp.zeros((HEAD_DIM, ts), BF16)
    for hc in range(2 * DA_HEADS):
        c = hc % 2
        piece = qt[hc * HEAD_DIM:(hc + 1) * HEAD_DIM].astype(BF16)
        qt_ref[hc * LANES + c * HEAD_DIM:hc * LANES + (c + 1) * HEAD_DIM, :] = piece
        qt_ref[hc * LANES + (1 - c) * HEAD_DIM:hc * LANES + (2 - c) * HEAD_DIM, :] = zero_half

    ka = rope(group(1), 1.0)
    ka_ref[...] = ka.astype(BF16)
    for h in range(DA_HEADS):
        kh = ka[:, h * LANES:(h + 1) * LANES]
        norm2 = jnp.max(jnp.sum(kh * kh, axis=1, keepdims=True), axis=0, keepdims=True)
        kn_ref[0, :, h * LANES:(h + 1) * LANES] = jnp.broadcast_to(norm2, (1, LANES))

    vt = lax.dot_general(wvt_ref[...], xn, (((1,), (1,)), ((), ())),
                         preferred_element_type=F32)
    pad_row = lax.broadcasted_iota(jnp.int32, (V_ROWS - DA_V, ts), 0)
    ones_pad = jnp.where(pad_row == 0, 1.0, 0.0).astype(BF16)
    for h in range(DA_HEADS):
        vt_ref[0, h * V_ROWS:h * V_ROWS + DA_V, :] = vt[h * DA_V:(h + 1) * DA_V].astype(BF16)
        vt_ref[0, h * V_ROWS + DA_V:(h + 1) * V_ROWS, :] = ones_pad

    qs_ref[...] = (group(3) * q_scale).astype(BF16)
    ks_ref[...] = group(4).astype(BF16)
    vs_ref[...] = group(5).astype(BF16)


def _in_proj(x, pos_row, invf, g, w, wvt, ts):
    S, D = x.shape
    nt = S // ts
    flat = jax.ShapeDtypeStruct((S, GROUP_COLS), BF16)
    row = lambda i: (i, 0)
    flat_spec = pl.BlockSpec((ts, GROUP_COLS), row)
    return pl.pallas_call(
        _in_proj_kernel,
        grid=(nt,),
        in_specs=[
            pl.BlockSpec((ts, D), row),
            pl.BlockSpec((1, ts), lambda i: (0, i)),
            _const_spec(invf.shape),
            _const_spec((1, D)),
            _const_spec(w.shape),
            _const_spec(wvt.shape),
        ],
        out_specs=[
            pl.BlockSpec((2 * DA_HEADS * LANES, ts), lambda i: (0, i)),
            flat_spec,
            pl.BlockSpec((1, DA_HEADS * V_ROWS, ts), lambda i: (i, 0, 0)),
            flat_spec, flat_spec, flat_spec,
            pl.BlockSpec((1, 1, DA_HEADS * LANES), lambda i: (i, 0, 0)),
        ],
        out_shape=[
            jax.ShapeDtypeStruct((2 * DA_HEADS * LANES, S), BF16),
            flat,
            jax.ShapeDtypeStruct((nt, DA_HEADS * V_ROWS, ts), BF16),
            flat, flat, flat,
            jax.ShapeDtypeStruct((nt, 1, DA_HEADS * LANES), F32),
        ],
        compiler_params=pltpu.CompilerParams(
            dimension_semantics=("arbitrary",), vmem_limit_bytes=VMEM_LIMIT),
        name="in_proj",
    )(x, pos_row, invf, g, w, wvt)


def _diff_attn_kernel(lam_ref, g_ref, kn_ref, qt_ref, k_ref, vt_ref, o_ref,
                      acc_ref, m_ref, pa_ref, pb_ref, ma_ref, mb_ref, fa_ref, fb_ref, *, lam_init):
    i = pl.program_id(1)
    t = qt_ref.shape[1]
    heads = qt_ref.shape[0] // (2 * LANES)
    maps = [(h, 2 * h + c) for h in range(heads) for c in range(2)]
    acc_ref[...] = jnp.zeros_like(acc_ref)

    def scores(j, h, mp, visible_offset):
        kb = k_ref[pl.ds(pl.multiple_of(j * t, t), t), h * LANES:(h + 1) * LANES]
        st = jnp.dot(kb, qt_ref[mp * LANES:(mp + 1) * LANES, :], preferred_element_type=F32)
        if visible_offset is not None:
            key = lax.broadcasted_iota(jnp.int32, st.shape, 0)
            qry = lax.broadcasted_iota(jnp.int32, st.shape, 1)
            st = jnp.where(key - qry <= visible_offset, st, NEG_BIG)
        return st

    def values(j, h):
        return vt_ref[j, h * V_ROWS:(h + 1) * V_ROWS, :]

    def exact_block(j, diagonal):
        for h, mp in maps:
            st = scores(j, h, mp, 0 if diagonal else None)
            m_old = m_ref[mp]
            m_new = jnp.maximum(m_old, jnp.max(st, axis=0, keepdims=True))
            alpha = jnp.exp2(m_old - m_new)
            p = jnp.exp2(st - m_new).astype(BF16)
            acc_ref[mp] = alpha * acc_ref[mp] + jnp.dot(values(j, h), p,
                                                        preferred_element_type=F32)
            m_ref[mp] = m_new

    def lagged_weights(j, p_ref, m_in_ref, m_out_ref, f_ref, visible_offset=None):
        for h, mp in maps:
            st = scores(j, h, mp, visible_offset)
            m_old = m_in_ref[mp]
            p_ref[mp] = jnp.exp2(st - m_old).astype(BF16)
            m_new = jnp.maximum(m_old, jnp.max(st, axis=0, keepdims=True))
            f_ref[mp] = jnp.exp2(m_old - m_new)
            m_out_ref[mp] = m_new

    def lagged_accumulate(j, p_ref, f_ref):
        for h, mp in maps:
            acc_ref[mp] = f_ref[mp] * (acc_ref[mp] + jnp.dot(values(j, h), p_ref[mp],
                                                            preferred_element_type=F32))

    tile = lax.broadcasted_iota(jnp.int32, kn_ref.shape, 0)
    kmax2 = jnp.max(jnp.where(tile <= i, kn_ref[...], 0.0)) * KEY_NORM_SLACK
    qn2 = jnp.zeros((1, t), F32)
    for _, mp in maps:
        qm = qt_ref[mp * LANES:(mp + 1) * LANES, :].astype(F32)
        qn2 = jnp.maximum(qn2, jnp.sum(qm * qm, axis=0, keepdims=True))
    safe = jnp.max(qn2) * kmax2 <= LAGGED_MAX_SCORE * LAGGED_MAX_SCORE

    @pl.when(safe)
    def _():
        mb_ref[...] = jnp.zeros_like(mb_ref)
        lagged_weights(0, pa_ref, mb_ref, ma_ref, fa_ref, visible_offset=i * t)

        def pair(n, carry):
            j = 2 * n
            lagged_weights(j + 1, pb_ref, ma_ref, mb_ref, fb_ref)
            lagged_accumulate(j, pa_ref, fa_ref)
            lagged_weights(j + 2, pa_ref, mb_ref, ma_ref, fa_ref, visible_offset=(i - j - 2) * t)
            lagged_accumulate(j + 1, pb_ref, fb_ref)
            return carry

        lax.fori_loop(0, i // 2, pair, 0)

        @pl.when(i % 2 == 1)
        def _():
            lagged_weights(i, pb_ref, ma_ref, mb_ref, fb_ref, visible_offset=0)
            lagged_accumulate(i - 1, pa_ref, fa_ref)
            lagged_accumulate(i, pb_ref, fb_ref)

        @pl.when(i % 2 == 0)
        def _():
            lagged_accumulate(i, pa_ref, fa_ref)

    @pl.when(jnp.logical_not(safe))
    def _():
        m_ref[...] = jnp.full_like(m_ref, NEG_BIG)

        def one(j, carry):
            exact_block(j, False)
            return carry

        lax.fori_loop(0, i, one, 0)
        exact_block(i, True)

    lam_v = lam_ref[...]
    lam = (jnp.exp(jnp.sum(lam_v[0:1] * lam_v[1:2], axis=1, keepdims=True))
           - jnp.exp(jnp.sum(lam_v[2:3] * lam_v[3:4], axis=1, keepdims=True)) + lam_init)
    for h in range(heads):
        a0 = acc_ref[2 * h]
        a1 = acc_ref[2 * h + 1]
        o = a0[:DA_V] / a0[DA_V:DA_V + 1] - lam * (a1[:DA_V] / a1[DA_V:DA_V + 1])
        ms = jnp.mean(o * o, axis=0, keepdims=True)
        y = o * lax.rsqrt(ms + EPS) * g_ref[...] * (1.0 - lam_init)
        o_ref[:, h * DA_V:(h + 1) * DA_V] = y.T.astype(o_ref.dtype)


def _diff_attn(lam_vecs, g_col, kn, qt, ka, vtblk, lam_init):
    S = ka.shape[0]
    nk, _, t = vtblk.shape
    hg = DIFF_HEADS_PER_STEP
    nm = 2 * hg
    col_block = lambda shape: pl.BlockSpec(shape, lambda g, i: (0,) * (len(shape) - 1) + (g,),
                                           pipeline_mode=pl.Buffered(1))
    return pl.pallas_call(
        functools.partial(_diff_attn_kernel, lam_init=lam_init),
        grid=(DA_HEADS // hg, nk),
        in_specs=[
            _const_spec(lam_vecs.shape),
            _const_spec(g_col.shape),
            col_block((nk, 1, hg * LANES)),
            pl.BlockSpec((nm * LANES, t), lambda g, i: (g, i)),
            col_block((S, hg * LANES)),
            pl.BlockSpec((nk, hg * V_ROWS, t), lambda g, i: (0, g, 0),
                         pipeline_mode=pl.Buffered(1)),
        ],
        out_specs=pl.BlockSpec((t, hg * DA_V), lambda g, i: (i, g)),
        out_shape=jax.ShapeDtypeStruct((S, DA_HEADS * DA_V), BF16),
        scratch_shapes=[pltpu.VMEM((nm, V_ROWS, t), F32), pltpu.VMEM((nm, 1, t), F32),
                        pltpu.VMEM((nm, t, t), BF16), pltpu.VMEM((nm, t, t), BF16),
                        pltpu.VMEM((nm, 1, t), F32), pltpu.VMEM((nm, 1, t), F32),
                        pltpu.VMEM((nm, 1, t), F32), pltpu.VMEM((nm, 1, t), F32)],
        compiler_params=pltpu.CompilerParams(
            dimension_semantics=("arbitrary", "arbitrary"), vmem_limit_bytes=VMEM_LIMIT),
        name="diff_attn",
    )(lam_vecs, g_col, kn, qt, ka, vtblk)


def _sb_attn_kernel(q_ref, k_ref, v_ref, o_ref, acc_ref, carry_ref):
    i = pl.program_id(0)
    t = q_ref.shape[0]
    pairs = q_ref.shape[1] // LANES
    lane = lax.broadcasted_iota(jnp.int32, (t, LANES), 1)
    first = lane < HEAD_DIM
    row = lax.broadcasted_iota(jnp.int32, (t, t), 0)
    col = lax.broadcasted_iota(jnp.int32, (t, t), 1)
    tri = jnp.where(row > col, 1.0, 0.0).astype(BF16)
    strict = jnp.concatenate([col < row, col < row], axis=0)

    def block(j, p, diagonal):
        lanes = slice(p * LANES, (p + 1) * LANES)
        q2 = q_ref[:, lanes]
        zeros = jnp.zeros_like(q2)
        qs = jnp.concatenate([jnp.where(first, q2, zeros), jnp.where(first, zeros, q2)], axis=0)
        rows = pl.ds(pl.multiple_of(j * t, t), t)
        kb = k_ref[rows, lanes]
        vb = v_ref[rows, lanes]
        z = lax.dot_general(qs, kb, (((1,), (1,)), ((), ())),
                            preferred_element_type=F32)
        sp = jnp.maximum(z, 0.0) + jnp.log2(1.0 + jnp.exp2(-jnp.abs(z)))
        if diagonal:
            sp = jnp.where(strict, sp, 0.0)
            carry = jnp.zeros((2 * t, 1), F32)
        else:
            carry = carry_ref[p]
        hi = sp.astype(BF16)
        between = jnp.dot(hi, tri, preferred_element_type=F32) + carry
        a = jnp.exp2(z - sp - between)
        if diagonal:
            a = jnp.where(strict, a, 0.0)
        av = jnp.dot(a.astype(BF16), vb, preferred_element_type=F32)
        acc_ref[p] = av if diagonal else acc_ref[p] + av
        carry = carry + jnp.sum(sp, axis=1, keepdims=True)
        carry_ref[p] = carry
        return carry

    def all_heads(j, diagonal):
        low = block(j, 0, diagonal)
        for p in range(1, pairs):
            low = jnp.minimum(low, block(j, p, diagonal))
        return jnp.min(low)

    def cond(state):
        return jnp.logical_and(state[0] >= 0, state[1] < SB_ZERO_LOG2)

    def body(state):
        j = state[0]
        return j - 1, all_heads(j, False)

    lax.while_loop(cond, body, (i - 1, all_heads(i, True)))
    for p in range(pairs):
        acc = acc_ref[p]
        o_ref[:, p * LANES:(p + 1) * LANES] = jnp.where(first, acc[:t], acc[t:]).astype(o_ref.dtype)


def _sb_attn(q, k, v, t):
    S, cols = q.shape
    pairs = cols // LANES
    return pl.pallas_call(
        _sb_attn_kernel,
        grid=(S // t,),
        in_specs=[
            pl.BlockSpec((t, cols), lambda i: (i, 0)),
            _const_spec(k.shape),
            _const_spec(v.shape),
        ],
        out_specs=pl.BlockSpec((t, cols), lambda i: (i, 0)),
        out_shape=jax.ShapeDtypeStruct((S, cols), BF16),
        scratch_shapes=[pltpu.VMEM((pairs, 2 * t, LANES), F32), pltpu.VMEM((pairs, 2 * t, 1), F32)],
        compiler_params=pltpu.CompilerParams(
            dimension_semantics=("arbitrary",), vmem_limit_bytes=VMEM_LIMIT),
        name="sb_attn",
    )(q, k, v)


def _mem_kv_kernel(mem_ref, g_ref, w_ref, kv_ref):
    mn = _rms(mem_ref[...], g_ref[...]).astype(BF16)
    kv_ref[...] = jnp.dot(mn, w_ref[...], preferred_element_type=F32).astype(kv_ref.dtype)


def _mem_kv(mem, g, w):
    M = mem.shape[0]
    return pl.pallas_call(
        _mem_kv_kernel,
        out_shape=jax.ShapeDtypeStruct((M, w.shape[1]), BF16),
        compiler_params=pltpu.CompilerParams(vmem_limit_bytes=VMEM_LIMIT),
        name="mem_kv",
    )(mem, g, w)


def _post_mlp_kernel(x_ref, ya_ref, yb_ref, wout_ref, gc_ref, wxq_ref, kv_ref, wxo_ref,
                     gm_ref, wup_ref, wdown_ref, gf_ref, o_ref, *, final_norm):
    D = x_ref.shape[1]
    half = ya_ref.shape[1]
    h = (x_ref[...]
         + jnp.dot(ya_ref[...], wout_ref[:half], preferred_element_type=F32)
         + jnp.dot(yb_ref[...], wout_ref[half:], preferred_element_type=F32))

    hn = _rms(h, gc_ref[...]).astype(BF16)
    xd = D // X_HEADS
    q = jnp.dot(hn, wxq_ref[...], preferred_element_type=F32) * (1.0 / math.sqrt(xd))
    q = q.astype(BF16)
    upd = jnp.zeros_like(h)
    for hd in range(X_HEADS):
        qh = q[:, hd * xd:(hd + 1) * xd]
        kh = kv_ref[:, hd * xd:(hd + 1) * xd]
        vh = kv_ref[:, D + hd * xd:D + (hd + 1) * xd]
        s = lax.dot_general(qh, kh, (((1,), (1,)), ((), ())), preferred_element_type=F32)
        p = jnp.exp(s - jnp.max(s, axis=-1, keepdims=True))
        l = jnp.sum(p, axis=-1, keepdims=True)
        oh = jnp.dot(p.astype(BF16), vh, preferred_element_type=F32) / l
        upd = upd + jnp.dot(oh.astype(BF16), wxo_ref[hd * xd:(hd + 1) * xd, :],
                            preferred_element_type=F32)
    h = h + upd

    hn = _rms(h, gm_ref[...]).astype(BF16)
    upd = jnp.zeros_like(h)
    for c in range(wup_ref.shape[1] // D):
        a = jnp.dot(hn, wup_ref[:, c * D:(c + 1) * D], preferred_element_type=F32)
        a = jnp.square(jnp.maximum(a, 0.0)).astype(BF16)
        upd = upd + jnp.dot(a, wdown_ref[c * D:(c + 1) * D, :], preferred_element_type=F32)
    h = h + upd
    if final_norm:
        h = _rms(h, gf_ref[...])
    o_ref[...] = h


def _post_mlp(x, ya, yb, kv, p, gf, final_norm):
    S, D = x.shape
    ts = min(ROW_TILE, S)
    row = lambda i: (i, 0)
    consts = [p["w_out"], p["g_cross"], p["w_xq"], kv, p["w_xo"],
              p["g_mlp"], p["w_up"], p["w_down"], gf]
    return pl.pallas_call(
        functools.partial(_post_mlp_kernel, final_norm=final_norm),
        grid=(S // ts,),
        in_specs=[pl.BlockSpec((ts, D), row),
                  pl.BlockSpec((ts, ya.shape[1]), row),
                  pl.BlockSpec((ts, yb.shape[1]), row)] + [_const_spec(c.shape) for c in consts],
        out_specs=pl.BlockSpec((ts, D), row),
        out_shape=jax.ShapeDtypeStruct((S, D), F32),
        compiler_params=pltpu.CompilerParams(
            dimension_semantics=("arbitrary",), vmem_limit_bytes=VMEM_LIMIT),
        name="post_mlp",
    )(x, ya, yb, *consts)


def _rope_inv_freq():
    inv = np.float32(ROPE_THETA) ** (-np.arange(0, ROT_DIM, 2, dtype=np.float32) / np.float32(ROT_DIM))
    return jnp.asarray(inv.reshape(ROT_DIM // 2, 1))


def _layer(h, pos_row, invf, mem, p, lam_init, gf, final_norm):
    S, D = h.shape
    td = min(DIFF_TILE, S)
    qt, ka, vtblk, qs, ks, vs, kn = _in_proj(h, pos_row, invf, p["g_mix"], p["w_in"], p["w_va_t"], td)
    ya = _diff_attn(p["lam_vecs"], p["g_subln"], kn, qt, ka, vtblk, lam_init)
    yb = _sb_attn(qs, ks, vs, min(SB_TILE, S))
    kv = _mem_kv(mem, p["g_mem"], p["w_xkv"])
    return _post_mlp(h, ya, yb, kv, p, gf, final_norm)


def kernel(x, mem, positions, g_mix, w_in, lambda_q1, lambda_k1, lambda_q2, lambda_k2, g_subln, w_out, g_cross, g_mem, w_xq, w_xkv, w_xo, g_mlp, w_up, w_down, g_final):
    B, S, D = x.shape
    depth = w_in.shape[0]
    invf = _rope_inv_freq()
    gf = g_final.reshape(1, D).astype(F32)
    outs = []
    for b in range(B):
        h = x[b]
        pos_row = positions[b].reshape(1, S)
        for l in range(depth):
            lam_init = 0.8 - 0.6 * math.exp(-0.3 * l)
            p = {
                "g_mix": g_mix[l].reshape(1, D),
                "w_in": w_in[l].astype(BF16),
                "w_va_t": w_in[l][:, 2 * GROUP_COLS:3 * GROUP_COLS].T.astype(BF16),
                "lam_vecs": jnp.stack([lambda_q1[l], lambda_k1[l], lambda_q2[l], lambda_k2[l]]).astype(F32),
                "g_subln": g_subln[l].reshape(DA_V, 1).astype(F32),
                "w_out": w_out[l].astype(BF16),
                "g_cross": g_cross[l].reshape(1, D),
                "g_mem": g_mem[l].reshape(1, D),
                "w_xq": w_xq[l].astype(BF16),
                "w_xkv": w_xkv[l].astype(BF16),
                "w_xo": w_xo[l].astype(BF16),
                "g_mlp": g_mlp[l].reshape(1, D),
                "w_up": w_up[l].astype(BF16),
                "w_down": w_down[l].astype(BF16),
            }
            h = _layer(h, pos_row, invf, mem[b], p, lam_init, gf, l == depth - 1)
        outs.append(h)
    return jnp.stack(outs)
```

```python
import functools
import math

import numpy as np
import jax
import jax.numpy as jnp
from jax import lax
from jax.experimental import pallas as pl
from jax.experimental.pallas import tpu as pltpu

F32 = jnp.float32
BF16 = jnp.bfloat16

HEAD_DIM = 64
DA_HEADS = 4
DA_V = 2 * HEAD_DIM
SB_HEADS = 8
GROUP_COLS = 512
ROT_DIM = HEAD_DIM // 4
ROPE_THETA = 500000.0
X_HEADS = 4
EPS = 1e-6
LANES = 128
NEG_BIG = -1e30

SB_ZERO_LOG2 = 151.0

LAGGED_MAX_SCORE = 45.0
KEY_NORM_SLACK = 1.02

VMEM_LIMIT = 56 * 1024 * 1024

ROW_TILE = 512
DIFF_TILE = 512
DIFF_HEADS_PER_STEP = 4
SB_TILE = 256
V_ROWS = DA_V + 16


def _rms(x, g):
    ms = jnp.mean(x * x, axis=-1, keepdims=True)
    return x * lax.rsqrt(ms + EPS) * g


def _const_spec(shape):
    nd = len(shape)
    return pl.BlockSpec(shape, lambda *_: (0,) * nd, pipeline_mode=pl.Buffered(1))


def _in_proj_kernel(x_ref, pos_ref, invf_ref, g_ref, w_ref, wvt_ref,
                    qt_ref, ka_ref, vt_ref, qs_ref, ks_ref, vs_ref, kn_ref):
    ts = x_ref.shape[0]
    xn = _rms(x_ref[...], g_ref[...]).astype(BF16)

    half = ROT_DIM // 2
    ang = invf_ref[...] * pos_ref[...].astype(F32)
    cos8 = jnp.cos(ang)
    sin8 = jnp.sin(ang)
    one8 = jnp.ones_like(cos8)
    zero8 = jnp.zeros_like(cos8)
    groups = range(LANES // half)
    in_x1 = [g % (HEAD_DIM // half) == 0 for g in groups]
    in_x2 = [g % (HEAD_DIM // half) == 1 for g in groups]
    cos = jnp.concatenate([cos8 if a or b else one8 for a, b in zip(in_x1, in_x2)], axis=0).T
    s_up = jnp.concatenate([-sin8 if a else zero8 for a in in_x1], axis=0).T
    s_dn = jnp.concatenate([sin8 if b else zero8 for b in in_x2], axis=0).T

    def group(idx):
        w = w_ref[:, idx * GROUP_COLS:(idx + 1) * GROUP_COLS]
        return jnp.dot(xn, w, preferred_element_type=F32)

    def rope(u, scale):
        parts = []
        for c in range(GROUP_COLS // LANES):
            uc = u[:, c * LANES:(c + 1) * LANES]
            r = (uc * cos + pltpu.roll(uc, LANES - ROT_DIM // 2, axis=1) * s_up
                 + pltpu.roll(uc, ROT_DIM // 2, axis=1) * s_dn)
            parts.append(r * scale)
        return jnp.concatenate(parts, axis=1)

    q_scale = math.log2(math.e) / math.sqrt(HEAD_DIM)

    qt = rope(group(0), q_scale).T
    zero_half = jnp.zeros((HEAD_DIM, ts), BF16)
    for hc in range(2 * DA_HEADS):
        c = hc % 2
        piece = qt[hc * HEAD_DIM:(hc + 1) * HEAD_DIM].astype(BF16)
        qt_ref[hc * LANES + c * HEAD_DIM:hc * LANES + (c + 1) * HEAD_DIM, :] = piece
        qt_ref[hc * LANES + (1 - c) * HEAD_DIM:hc * LANES + (2 - c) * HEAD_DIM, :] = zero_half

    ka = rope(group(1), 1.0)
    ka_ref[...] = ka.astype(BF16)
    for h in range(DA_HEADS):
        kh = ka[:, h * LANES:(h + 1) * LANES]
        norm2 = jnp.max(jnp.sum(kh * kh, axis=1, keepdims=True), axis=0, keepdims=True)
        kn_ref[0, :, h * LANES:(h + 1) * LANES] = jnp.broadcast_to(norm2, (1, LANES))

    vt = lax.dot_general(wvt_ref[...], xn, (((1,), (1,)), ((), ())),
                         preferred_element_type=F32)
    pad_row = lax.broadcasted_iota(jnp.int32, (V_ROWS - DA_V, ts), 0)
    ones_pad = jnp.where(pad_row == 0, 1.0, 0.0).astype(BF16)
    for h in range(DA_HEADS):
        vt_ref[0, h * V_ROWS:h * V_ROWS + DA_V, :] = vt[h * DA_V:(h + 1) * DA_V].astype(BF16)
        vt_ref[0, h * V_ROWS + DA_V:(h + 1) * V_ROWS, :] = ones_pad

    qs_ref[...] = (group(3) * q_scale).astype(BF16)
    ks_ref[...] = group(4).astype(BF16)
    vs_ref[...] = group(5).astype(BF16)


def _in_proj(x, pos_row, invf, g, w, wvt, ts):
    S, D = x.shape
    nt = S // ts
    flat = jax.ShapeDtypeStruct((S, GROUP_COLS), BF16)
    row = lambda i: (i, 0)
    flat_spec = pl.BlockSpec((ts, GROUP_COLS), row)
    return pl.pallas_call(
        _in_proj_kernel,
        grid=(nt,),
        in_specs=[
            pl.BlockSpec((ts, D), row),
            pl.BlockSpec((1, ts), lambda i: (0, i)),
            _const_spec(invf.shape),
            _const_spec((1, D)),
            _const_spec(w.shape),
            _const_spec(wvt.shape),
        ],
        out_specs=[
            pl.BlockSpec((2 * DA_HEADS * LANES, ts), lambda i: (0, i)),
            flat_spec,
            pl.BlockSpec((1, DA_HEADS * V_ROWS, ts), lambda i: (i, 0, 0)),
            flat_spec, flat_spec, flat_spec,
            pl.BlockSpec((1, 1, DA_HEADS * LANES), lambda i: (i, 0, 0)),
        ],
        out_shape=[
            jax.ShapeDtypeStruct((2 * DA_HEADS * LANES, S), BF16),
            flat,
            jax.ShapeDtypeStruct((nt, DA_HEADS * V_ROWS, ts), BF16),
            flat, flat, flat,
            jax.ShapeDtypeStruct((nt, 1, DA_HEADS * LANES), F32),
        ],
        compiler_params=pltpu.CompilerParams(
            dimension_semantics=("arbitrary",), vmem_limit_bytes=VMEM_LIMIT),
        name="in_proj",
    )(x, pos_row, invf, g, w, wvt)


def _diff_attn_kernel(lam_ref, g_ref, kn_ref, qt_ref, k_ref, vt_ref, o_ref,
                      acc_ref, m_ref, pa_ref, pb_ref, ma_ref, mb_ref, fa_ref, fb_ref, *, lam_init):
    i = pl.program_id(1)
    t = qt_ref.shape[1]
    heads = qt_ref.shape[0] // (2 * LANES)
    maps = [(h, 2 * h + c) for h in range(heads) for c in range(2)]
    acc_ref[...] = jnp.zeros_like(acc_ref)
    diag_offset = jnp.minimum(i, 0)

    def scores(j, h, mp, visible_offset):
        kb = k_ref[pl.ds(pl.multiple_of(j * t, t), t), h * LANES:(h + 1) * LANES]
        st = jnp.dot(kb, qt_ref[mp * LANES:(mp + 1) * LANES, :], preferred_element_type=F32)
        if visible_offset is not None:
            key = lax.broadcasted_iota(jnp.int32, st.shape, 0)
            qry = lax.broadcasted_iota(jnp.int32, st.shape, 1)
            st = jnp.where(key - qry <= visible_offset, st, NEG_BIG)
        return st

    def values(j, h):
        return vt_ref[j, h * V_ROWS:(h + 1) * V_ROWS, :]

    def exact_block(j, diagonal):
        for h, mp in maps:
            st = scores(j, h, mp, diag_offset if diagonal else None)
            m_old = m_ref[mp]
            m_new = jnp.maximum(m_old, jnp.max(st, axis=0, keepdims=True))
            alpha = jnp.exp2(m_old - m_new)
            p = jnp.exp2(st - m_new).astype(BF16)
            acc_ref[mp] = alpha * acc_ref[mp] + jnp.dot(values(j, h), p,
                                                        preferred_element_type=F32)
            m_ref[mp] = m_new

    def lagged_weights(j, p_ref, m_in_ref, m_out_ref, f_ref, visible_offset=None):
        for h, mp in maps:
            st = scores(j, h, mp, visible_offset)
            m_old = m_in_ref[mp]
            p_ref[mp] = jnp.exp2(st - m_old).astype(BF16)
            m_new = jnp.maximum(m_old, jnp.max(st, axis=0, keepdims=True))
            f_ref[mp] = jnp.exp2(m_old - m_new)
            m_out_ref[mp] = m_new

    def lagged_accumulate(j, p_ref, f_ref):
        for h, mp in maps:
            acc_ref[mp] = f_ref[mp] * (acc_ref[mp] + jnp.dot(values(j, h), p_ref[mp],
                                                            preferred_element_type=F32))

    tile = lax.broadcasted_iota(jnp.int32, kn_ref.shape, 0)
    kmax2 = jnp.max(jnp.where(tile <= i, kn_ref[...], 0.0)) * KEY_NORM_SLACK
    qn2 = jnp.zeros((1, t), F32)
    for _, mp in maps:
        qm = qt_ref[mp * LANES:(mp + 1) * LANES, :].astype(F32)
        qn2 = jnp.maximum(qn2, jnp.sum(qm * qm, axis=0, keepdims=True))
    safe = jnp.max(qn2) * kmax2 <= LAGGED_MAX_SCORE * LAGGED_MAX_SCORE

    @pl.when(safe)
    def _():
        mb_ref[...] = jnp.zeros_like(mb_ref)

        @pl.when(i == 0)
        def _():
            lagged_weights(0, pa_ref, mb_ref, ma_ref, fa_ref, visible_offset=diag_offset)
            lagged_accumulate(0, pa_ref, fa_ref)

        @pl.when(i > 0)
        def _():
            lagged_weights(0, pa_ref, mb_ref, ma_ref, fa_ref)

            def pair(n, carry):
                j = 2 * n
                lagged_weights(j + 1, pb_ref, ma_ref, mb_ref, fb_ref)
                lagged_accumulate(j, pa_ref, fa_ref)
                lagged_weights(j + 2, pa_ref, mb_ref, ma_ref, fa_ref)
                lagged_accumulate(j + 1, pb_ref, fb_ref)
                return carry

            lax.fori_loop(0, (i - 1) // 2, pair, 0)

            @pl.when(i % 2 == 1)
            def _():
                lagged_weights(i, pb_ref, ma_ref, mb_ref, fb_ref, visible_offset=diag_offset)
                lagged_accumulate(i - 1, pa_ref, fa_ref)
                lagged_accumulate(i, pb_ref, fb_ref)

            @pl.when(i % 2 == 0)
            def _():
                lagged_weights(i - 1, pb_ref, ma_ref, mb_ref, fb_ref)
                lagged_accumulate(i - 2, pa_ref, fa_ref)
                lagged_weights(i, pa_ref, mb_ref, ma_ref, fa_ref, visible_offset=diag_offset)
                lagged_accumulate(i - 1, pb_ref, fb_ref)
                lagged_accumulate(i, pa_ref, fa_ref)

    @pl.when(jnp.logical_not(safe))
    def _():
        m_ref[...] = jnp.full_like(m_ref, NEG_BIG)

        def one(j, carry):
            exact_block(j, False)
            return carry

        lax.fori_loop(0, i, one, 0)
        exact_block(i, True)

    lam_v = lam_ref[...]
    lam = (jnp.exp(jnp.sum(lam_v[0:1] * lam_v[1:2], axis=1, keepdims=True))
           - jnp.exp(jnp.sum(lam_v[2:3] * lam_v[3:4], axis=1, keepdims=True)) + lam_init)
    for h in range(heads):
        a0 = acc_ref[2 * h]
        a1 = acc_ref[2 * h + 1]
        o = a0[:DA_V] / a0[DA_V:DA_V + 1] - lam * (a1[:DA_V] / a1[DA_V:DA_V + 1])
        ms = jnp.mean(o * o, axis=0, keepdims=True)
        y = o * lax.rsqrt(ms + EPS) * g_ref[...] * (1.0 - lam_init)
        o_ref[:, h * DA_V:(h + 1) * DA_V] = y.T.astype(o_ref.dtype)


def _diff_attn(lam_vecs, g_col, kn, qt, ka, vtblk, lam_init):
    S = ka.shape[0]
    nk, _, t = vtblk.shape
    hg = DIFF_HEADS_PER_STEP
    nm = 2 * hg
    col_block = lambda shape: pl.BlockSpec(shape, lambda g, i: (0,) * (len(shape) - 1) + (g,),
                                           pipeline_mode=pl.Buffered(1))
    return pl.pallas_call(
        functools.partial(_diff_attn_kernel, lam_init=lam_init),
        grid=(DA_HEADS // hg, nk),
        in_specs=[
            _const_spec(lam_vecs.shape),
            _const_spec(g_col.shape),
            col_block((nk, 1, hg * LANES)),
            pl.BlockSpec((nm * LANES, t), lambda g, i: (g, i)),
            col_block((S, hg * LANES)),
            pl.BlockSpec((nk, hg * V_ROWS, t), lambda g, i: (0, g, 0),
                         pipeline_mode=pl.Buffered(1)),
        ],
        out_specs=pl.BlockSpec((t, hg * DA_V), lambda g, i: (i, g)),
        out_shape=jax.ShapeDtypeStruct((S, DA_HEADS * DA_V), BF16),
        scratch_shapes=[pltpu.VMEM((nm, V_ROWS, t), F32), pltpu.VMEM((nm, 1, t), F32),
                        pltpu.VMEM((nm, t, t), BF16), pltpu.VMEM((nm, t, t), BF16),
                        pltpu.VMEM((nm, 1, t), F32), pltpu.VMEM((nm, 1, t), F32),
                        pltpu.VMEM((nm, 1, t), F32), pltpu.VMEM((nm, 1, t), F32)],
        compiler_params=pltpu.CompilerParams(
            dimension_semantics=("arbitrary", "arbitrary"), vmem_limit_bytes=VMEM_LIMIT),
        name="diff_attn",
    )(lam_vecs, g_col, kn, qt, ka, vtblk)


def _sb_attn_kernel(q_ref, k_ref, v_ref, o_ref, acc_ref, carry_ref):
    i = pl.program_id(0)
    t = q_ref.shape[0]
    pairs = q_ref.shape[1] // LANES
    lane = lax.broadcasted_iota(jnp.int32, (t, LANES), 1)
    first = lane < HEAD_DIM
    row = lax.broadcasted_iota(jnp.int32, (t, t), 0)
    col = lax.broadcasted_iota(jnp.int32, (t, t), 1)
    tri = jnp.where(row > col, 1.0, 0.0).astype(BF16)
    strict = jnp.concatenate([col < row, col < row], axis=0)

    def block(j, p, diagonal):
        lanes = slice(p * LANES, (p + 1) * LANES)
        q2 = q_ref[:, lanes]
        zeros = jnp.zeros_like(q2)
        qs = jnp.concatenate([jnp.where(first, q2, zeros), jnp.where(first, zeros, q2)], axis=0)
        rows = pl.ds(pl.multiple_of(j * t, t), t)
        kb = k_ref[rows, lanes]
        vb = v_ref[rows, lanes]
        z = lax.dot_general(qs, kb, (((1,), (1,)), ((), ())),
                            preferred_element_type=F32)
        sp = jnp.maximum(z, 0.0) + jnp.log2(1.0 + jnp.exp2(-jnp.abs(z)))
        if diagonal:
            sp = jnp.where(strict, sp, 0.0)
            carry = jnp.zeros((2 * t, 1), F32)
        else:
            carry = carry_ref[p]
        hi = sp.astype(BF16)
        between = jnp.dot(hi, tri, preferred_element_type=F32) + carry
        a = jnp.exp2(z - sp - between)
        if diagonal:
            a = jnp.where(strict, a, 0.0)
        av = jnp.dot(a.astype(BF16), vb, preferred_element_type=F32)
        acc_ref[p] = av if diagonal else acc_ref[p] + av
        carry = carry + jnp.sum(sp, axis=1, keepdims=True)
        carry_ref[p] = carry
        return carry

    def all_heads(j, diagonal):
        low = block(j, 0, diagonal)
        for p in range(1, pairs):
            low = jnp.minimum(low, block(j, p, diagonal))
        return jnp.min(low)

    def cond(state):
        return jnp.logical_and(state[0] >= 0, state[1] < SB_ZERO_LOG2)

    def body(state):
        j = state[0]
        return j - 1, all_heads(j, False)

    lax.while_loop(cond, body, (i - 1, all_heads(i, True)))
    for p in range(pairs):
        acc = acc_ref[p]
        o_ref[:, p * LANES:(p + 1) * LANES] = jnp.where(first, acc[:t], acc[t:]).astype(o_ref.dtype)


def _sb_attn(q, k, v, t):
    S, cols = q.shape
    pairs = cols // LANES
    return pl.pallas_call(
        _sb_attn_kernel,
        grid=(S // t,),
        in_specs=[
            pl.BlockSpec((t, cols), lambda i: (i, 0)),
            _const_spec(k.shape),
            _const_spec(v.shape),
        ],
        out_specs=pl.BlockSpec((t, cols), lambda i: (i, 0)),
        out_shape=jax.ShapeDtypeStruct((S, cols), BF16),
        scratch_shapes=[pltpu.VMEM((pairs, 2 * t, LANES), F32), pltpu.VMEM((pairs, 2 * t, 1), F32)],
        compiler_params=pltpu.CompilerParams(
            dimension_semantics=("arbitrary",), vmem_limit_bytes=VMEM_LIMIT),
        name="sb_attn",
    )(q, k, v)


def _mem_kv_kernel(mem_ref, g_ref, w_ref, kv_ref):
    mn = _rms(mem_ref[...], g_ref[...]).astype(BF16)
    kv_ref[...] = jnp.dot(mn, w_ref[...], preferred_element_type=F32).astype(kv_ref.dtype)


def _mem_kv(mem, g, w):
    M = mem.shape[0]
    return pl.pallas_call(
        _mem_kv_kernel,
        out_shape=jax.ShapeDtypeStruct((M, w.shape[1]), BF16),
        compiler_params=pltpu.CompilerParams(vmem_limit_bytes=VMEM_LIMIT),
        name="mem_kv",
    )(mem, g, w)


def _post_mlp_kernel(x_ref, ya_ref, yb_ref, wout_ref, gc_ref, wxq_ref, kv_ref, wxo_ref,
                     gm_ref, wup_ref, wdown_ref, gf_ref, o_ref, *, final_norm):
    D = x_ref.shape[1]
    half = ya_ref.shape[1]
    h = (x_ref[...]
         + jnp.dot(ya_ref[...], wout_ref[:half], preferred_element_type=F32)
         + jnp.dot(yb_ref[...], wout_ref[half:], preferred_element_type=F32))

    hn = _rms(h, gc_ref[...]).astype(BF16)
    xd = D // X_HEADS
    q = jnp.dot(hn, wxq_ref[...], preferred_element_type=F32) * (1.0 / math.sqrt(xd))
    q = q.astype(BF16)
    upd = jnp.zeros_like(h)
    for hd in range(X_HEADS):
        qh = q[:, hd * xd:(hd + 1) * xd]
        kh = kv_ref[:, hd * xd:(hd + 1) * xd]
        vh = kv_ref[:, D + hd * xd:D + (hd + 1) * xd]
        s = lax.dot_general(qh, kh, (((1,), (1,)), ((), ())), preferred_element_type=F32)
        p = jnp.exp(s - jnp.max(s, axis=-1, keepdims=True))
        l = jnp.sum(p, axis=-1, keepdims=True)
        oh = jnp.dot(p.astype(BF16), vh, preferred_element_type=F32) / l
        upd = upd + jnp.dot(oh.astype(BF16), wxo_ref[hd * xd:(hd + 1) * xd, :],
                            preferred_element_type=F32)
    h = h + upd

    hn = _rms(h, gm_ref[...]).astype(BF16)
    upd = jnp.zeros_like(h)
    for c in range(wup_ref.shape[1] // D):
        a = jnp.dot(hn, wup_ref[:, c * D:(c + 1) * D], preferred_element_type=F32)
        a = jnp.square(jnp.maximum(a, 0.0)).astype(BF16)
        upd = upd + jnp.dot(a, wdown_ref[c * D:(c + 1) * D, :], preferred_element_type=F32)
    h = h + upd
    if final_norm:
        h = _rms(h, gf_ref[...])
    o_ref[...] = h


def _post_mlp(x, ya, yb, kv, p, gf, final_norm):
    S, D = x.shape
    ts = min(ROW_TILE, S)
    row = lambda i: (i, 0)
    consts = [p["w_out"], p["g_cross"], p["w_xq"], kv, p["w_xo"],
              p["g_mlp"], p["w_up"], p["w_down"], gf]
    return pl.pallas_call(
        functools.partial(_post_mlp_kernel, final_norm=final_norm),
        grid=(S // ts,),
        in_specs=[pl.BlockSpec((ts, D), row),
                  pl.BlockSpec((ts, ya.shape[1]), row),
                  pl.BlockSpec((ts, yb.shape[1]), row)] + [_const_spec(c.shape) for c in consts],
        out_specs=pl.BlockSpec((ts, D), row),
        out_shape=jax.ShapeDtypeStruct((S, D), F32),
        compiler_params=pltpu.CompilerParams(
            dimension_semantics=("arbitrary",), vmem_limit_bytes=VMEM_LIMIT),
        name="post_mlp",
    )(x, ya, yb, *consts)


def _rope_inv_freq():
    inv = np.float32(ROPE_THETA) ** (-np.arange(0, ROT_DIM, 2, dtype=np.float32) / np.float32(ROT_DIM))
    return jnp.asarray(inv.reshape(ROT_DIM // 2, 1))


def _layer(h, pos_row, invf, mem, p, lam_init, gf, final_norm):
    S, D = h.shape
    td = min(DIFF_TILE, S)
    qt, ka, vtblk, qs, ks, vs, kn = _in_proj(h, pos_row, invf, p["g_mix"], p["w_in"], p["w_va_t"], td)
    ya = _diff_attn(p["lam_vecs"], p["g_subln"], kn, qt, ka, vtblk, lam_init)
    yb = _sb_attn(qs, ks, vs, min(SB_TILE, S))
    kv = _mem_kv(mem, p["g_mem"], p["w_xkv"])
    return _post_mlp(h, ya, yb, kv, p, gf, final_norm)


def kernel(x, mem, positions, g_mix, w_in, lambda_q1, lambda_k1, lambda_q2, lambda_k2, g_subln, w_out, g_cross, g_mem, w_xq, w_xkv, w_xo, g_mlp, w_up, w_down, g_final):
    B, S, D = x.shape
    depth = w_in.shape[0]
    invf = _rope_inv_freq()
    gf = g_final.reshape(1, D).astype(F32)
    outs = []
    for b in range(B):
        h = x[b]
        pos_row = positions[b].reshape(1, S)
        for l in range(depth):
            lam_init = 0.8 - 0.6 * math.exp(-0.3 * l)
            p = {
                "g_mix": g_mix[l].reshape(1, D),
                "w_in": w_in[l].astype(BF16),
                "w_va_t": w_in[l][:, 2 * GROUP_COLS:3 * GROUP_COLS].T.astype(BF16),
                "lam_vecs": jnp.stack([lambda_q1[l], lambda_k1[l], lambda_q2[l], lambda_k2[l]]).astype(F32),
                "g_subln": g_subln[l].reshape(DA_V, 1).astype(F32),
                "w_out": w_out[l].astype(BF16),
                "g_cross": g_cross[l].reshape(1, D),
                "g_mem": g_mem[l].reshape(1, D),
                "w_xq": w_xq[l].astype(BF16),
                "w_xkv": w_xkv[l].astype(BF16),
                "w_xo": w_xo[l].astype(BF16),
                "g_mlp": g_mlp[l].reshape(1, D),
                "w_up": w_up[l].astype(BF16),
                "w_down": w_down[l].astype(BF16),
            }
            h = _layer(h, pos_row, invf, mem[b], p, lam_init, gf, l == depth - 1)
        outs.append(h)
    return jnp.stack(outs)
```

```python
import functools
import math

import numpy as np
import jax
import jax.numpy as jnp
from jax import lax
from jax.experimental import pallas as pl
from jax.experimental.pallas import tpu as pltpu

F32 = jnp.float32
BF16 = jnp.bfloat16

HEAD_DIM = 64
DA_HEADS = 4
DA_V = 2 * HEAD_DIM
SB_HEADS = 8
GROUP_COLS = 512
ROT_DIM = HEAD_DIM // 4
ROPE_THETA = 500000.0
X_HEADS = 4
EPS = 1e-6
LANES = 128
NEG_BIG = -1e30

SB_ZERO_LOG2 = 151.0
SB_LINEAR_ABOVE = 64.0

LAGGED_MAX_SCORE = 45.0
KEY_NORM_SLACK = 1.02

VMEM_LIMIT = 56 * 1024 * 1024

ROW_TILE = 512
DIFF_TILE = 512
DIFF_HEADS_PER_STEP = 4
SB_TILE = 256
V_ROWS = DA_V + 16


def _rms(x, g):
    ms = jnp.mean(x * x, axis=-1, keepdims=True)
    return x * lax.rsqrt(ms + EPS) * g


def _const_spec(shape):
    nd = len(shape)
    return pl.BlockSpec(shape, lambda *_: (0,) * nd, pipeline_mode=pl.Buffered(1))


def _in_proj_kernel(x_ref, pos_ref, invf_ref, g_ref, w_ref,
                    qt_ref, ka_ref, vt_ref, qs_ref, ks_ref, vs_ref, kn_ref):
    ts = x_ref.shape[0]
    xn = _rms(x_ref[...], g_ref[...]).astype(BF16)

    half = ROT_DIM // 2
    ang = invf_ref[...] * pos_ref[...].astype(F32)
    cos8 = jnp.cos(ang)
    sin8 = jnp.sin(ang)
    one8 = jnp.ones_like(cos8)
    zero8 = jnp.zeros_like(cos8)
    groups = range(LANES // half)
    in_x1 = [g % (HEAD_DIM // half) == 0 for g in groups]
    in_x2 = [g % (HEAD_DIM // half) == 1 for g in groups]
    cos = jnp.concatenate([cos8 if a or b else one8 for a, b in zip(in_x1, in_x2)], axis=0).T
    s_up = jnp.concatenate([-sin8 if a else zero8 for a in in_x1], axis=0).T
    s_dn = jnp.concatenate([sin8 if b else zero8 for b in in_x2], axis=0).T

    def group(idx):
        w = w_ref[:, idx * GROUP_COLS:(idx + 1) * GROUP_COLS]
        return jnp.dot(xn, w, preferred_element_type=F32)

    def rope(u, scale):
        parts = []
        for c in range(GROUP_COLS // LANES):
            uc = u[:, c * LANES:(c + 1) * LANES]
            r = (uc * cos + pltpu.roll(uc, LANES - ROT_DIM // 2, axis=1) * s_up
                 + pltpu.roll(uc, ROT_DIM // 2, axis=1) * s_dn)
            parts.append(r * scale)
        return jnp.concatenate(parts, axis=1)

    q_scale = math.log2(math.e) / math.sqrt(HEAD_DIM)

    qt = rope(group(0), q_scale).T
    zero_half = jnp.zeros((HEAD_DIM, ts), BF16)
    for hc in range(2 * DA_HEADS):
        c = hc % 2
        piece = qt[hc * HEAD_DIM:(hc + 1) * HEAD_DIM].astype(BF16)
        qt_ref[hc * LANES + c * HEAD_DIM:hc * LANES + (c + 1) * HEAD_DIM, :] = piece
        qt_ref[hc * LANES + (1 - c) * HEAD_DIM:hc * LANES + (2 - c) * HEAD_DIM, :] = zero_half

    ka = rope(group(1), 1.0)
    ka_ref[...] = ka.astype(BF16)
    for h in range(DA_HEADS):
        kh = ka[:, h * LANES:(h + 1) * LANES]
        norm2 = jnp.max(jnp.sum(kh * kh, axis=1, keepdims=True), axis=0, keepdims=True)
        kn_ref[0, :, h * LANES:(h + 1) * LANES] = jnp.broadcast_to(norm2, (1, LANES))

    vt = group(2).astype(BF16).T
    pad_row = lax.broadcasted_iota(jnp.int32, (V_ROWS - DA_V, ts), 0)
    ones_pad = jnp.where(pad_row == 0, 1.0, 0.0).astype(BF16)
    for h in range(DA_HEADS):
        vt_ref[0, h * V_ROWS:h * V_ROWS + DA_V, :] = vt[h * DA_V:(h + 1) * DA_V]
        vt_ref[0, h * V_ROWS + DA_V:(h + 1) * V_ROWS, :] = ones_pad

    qs_ref[...] = (group(3) * q_scale).astype(BF16)
    ks_ref[...] = group(4).astype(BF16)
    vs_ref[...] = group(5).astype(BF16)


def _in_proj(x, pos_row, invf, g, w, ts):
    S, D = x.shape
    nt = S // ts
    flat = jax.ShapeDtypeStruct((S, GROUP_COLS), BF16)
    row = lambda i: (i, 0)
    flat_spec = pl.BlockSpec((ts, GROUP_COLS), row)
    return pl.pallas_call(
        _in_proj_kernel,
        grid=(nt,),
        in_specs=[
            pl.BlockSpec((ts, D), row),
            pl.BlockSpec((1, ts), lambda i: (0, i)),
            _const_spec(invf.shape),
            _const_spec((1, D)),
            _const_spec(w.shape),
        ],
        out_specs=[
            pl.BlockSpec((2 * DA_HEADS * LANES, ts), lambda i: (0, i)),
            flat_spec,
            pl.BlockSpec((1, DA_HEADS * V_ROWS, ts), lambda i: (i, 0, 0)),
            flat_spec, flat_spec, flat_spec,
            pl.BlockSpec((1, 1, DA_HEADS * LANES), lambda i: (i, 0, 0)),
        ],
        out_shape=[
            jax.ShapeDtypeStruct((2 * DA_HEADS * LANES, S), BF16),
            flat,
            jax.ShapeDtypeStruct((nt, DA_HEADS * V_ROWS, ts), BF16),
            flat, flat, flat,
            jax.ShapeDtypeStruct((nt, 1, DA_HEADS * LANES), F32),
        ],
        compiler_params=pltpu.CompilerParams(
            dimension_semantics=("arbitrary",), vmem_limit_bytes=VMEM_LIMIT),
        name="in_proj",
    )(x, pos_row, invf, g, w)


def _diff_attn_kernel(lam_ref, g_ref, kn_ref, qt_ref, k_ref, vt_ref, o_ref,
                      acc_ref, m_ref, pa_ref, pb_ref, ma_ref, mb_ref, fa_ref, fb_ref, *, lam_init):
    i = pl.program_id(1)
    t = qt_ref.shape[1]
    heads = qt_ref.shape[0] // (2 * LANES)
    maps = [(h, 2 * h + c) for h in range(heads) for c in range(2)]
    acc_ref[...] = jnp.zeros_like(acc_ref)
    diag_offset = jnp.minimum(i, 0)

    def scores(j, h, mp, visible_offset):
        kb = k_ref[pl.ds(pl.multiple_of(j * t, t), t), h * LANES:(h + 1) * LANES]
        st = jnp.dot(kb, qt_ref[mp * LANES:(mp + 1) * LANES, :], preferred_element_type=F32)
        if visible_offset is not None:
            key = lax.broadcasted_iota(jnp.int32, st.shape, 0)
            qry = lax.broadcasted_iota(jnp.int32, st.shape, 1)
            st = jnp.where(key - qry <= visible_offset, st, NEG_BIG)
        return st

    def values(j, h):
        return vt_ref[j, h * V_ROWS:(h + 1) * V_ROWS, :]

    def exact_block(j, diagonal):
        for h, mp in maps:
            st = scores(j, h, mp, diag_offset if diagonal else None)
            m_old = m_ref[mp]
            m_new = jnp.maximum(m_old, jnp.max(st, axis=0, keepdims=True))
            alpha = jnp.exp2(m_old - m_new)
            p = jnp.exp2(st - m_new).astype(BF16)
            acc_ref[mp] = alpha * acc_ref[mp] + jnp.dot(values(j, h), p,
                                                        preferred_element_type=F32)
            m_ref[mp] = m_new

    def lagged_weights(j, p_ref, m_in_ref, m_out_ref, f_ref, visible_offset=None):
        for h, mp in maps:
            st = scores(j, h, mp, visible_offset)
            m_old = m_in_ref[mp]
            p_ref[mp] = jnp.exp2(st - m_old).astype(BF16)
            m_new = jnp.maximum(m_old, jnp.max(st, axis=0, keepdims=True))
            f_ref[mp] = jnp.exp2(m_old - m_new)
            m_out_ref[mp] = m_new

    def lagged_accumulate(j, p_ref, f_ref):
        for h, mp in maps:
            acc_ref[mp] = f_ref[mp] * (acc_ref[mp] + jnp.dot(values(j, h), p_ref[mp],
                                                            preferred_element_type=F32))

    tile = lax.broadcasted_iota(jnp.int32, kn_ref.shape, 0)
    kmax2 = jnp.max(jnp.where(tile <= i, kn_ref[...], 0.0)) * KEY_NORM_SLACK
    qn2 = jnp.zeros((1, t), F32)
    for _, mp in maps:
        qm = qt_ref[mp * LANES:(mp + 1) * LANES, :].astype(F32)
        qn2 = jnp.maximum(qn2, jnp.sum(qm * qm, axis=0, keepdims=True))
    safe = jnp.max(qn2) * kmax2 <= LAGGED_MAX_SCORE * LAGGED_MAX_SCORE

    @pl.when(safe)
    def _():
        mb_ref[...] = jnp.zeros_like(mb_ref)

        @pl.when(i == 0)
        def _():
            lagged_weights(0, pa_ref, mb_ref, ma_ref, fa_ref, visible_offset=diag_offset)
            lagged_accumulate(0, pa_ref, fa_ref)

        @pl.when(i > 0)
        def _():
            lagged_weights(0, pa_ref, mb_ref, ma_ref, fa_ref)

            def pair(n, carry):
                j = 2 * n
                lagged_weights(j + 1, pb_ref, ma_ref, mb_ref, fb_ref)
                lagged_accumulate(j, pa_ref, fa_ref)
                lagged_weights(j + 2, pa_ref, mb_ref, ma_ref, fa_ref)
                lagged_accumulate(j + 1, pb_ref, fb_ref)
                return carry

            lax.fori_loop(0, (i - 1) // 2, pair, 0)

            @pl.when(i % 2 == 1)
            def _():
                lagged_weights(i, pb_ref, ma_ref, mb_ref, fb_ref, visible_offset=diag_offset)
                lagged_accumulate(i - 1, pa_ref, fa_ref)
                lagged_accumulate(i, pb_ref, fb_ref)

            @pl.when(i % 2 == 0)
            def _():
                lagged_weights(i - 1, pb_ref, ma_ref, mb_ref, fb_ref)
                lagged_accumulate(i - 2, pa_ref, fa_ref)
                lagged_weights(i, pa_ref, mb_ref, ma_ref, fa_ref, visible_offset=diag_offset)
                lagged_accumulate(i - 1, pb_ref, fb_ref)
                lagged_accumulate(i, pa_ref, fa_ref)

    @pl.when(jnp.logical_not(safe))
    def _():
        m_ref[...] = jnp.full_like(m_ref, NEG_BIG)

        def one(j, carry):
            exact_block(j, False)
            return carry

        lax.fori_loop(0, i, one, 0)
        exact_block(i, True)

    lam_v = lam_ref[...]
    lam = (jnp.exp(jnp.sum(lam_v[0:1] * lam_v[1:2], axis=1, keepdims=True))
           - jnp.exp(jnp.sum(lam_v[2:3] * lam_v[3:4], axis=1, keepdims=True)) + lam_init)
    for h in range(heads):
        a0 = acc_ref[2 * h]
        a1 = acc_ref[2 * h + 1]
        o = a0[:DA_V] / a0[DA_V:DA_V + 1] - lam * (a1[:DA_V] / a1[DA_V:DA_V + 1])
        ms = jnp.mean(o * o, axis=0, keepdims=True)
        y = o * lax.rsqrt(ms + EPS) * g_ref[...] * (1.0 - lam_init)
        o_ref[:, h * DA_V:(h + 1) * DA_V] = y.T.astype(o_ref.dtype)


def _diff_attn(lam_vecs, g_col, kn, qt, ka, vtblk, lam_init):
    S = ka.shape[0]
    nk, _, t = vtblk.shape
    hg = DIFF_HEADS_PER_STEP
    nm = 2 * hg
    col_block = lambda shape: pl.BlockSpec(shape, lambda g, i: (0,) * (len(shape) - 1) + (g,),
                                           pipeline_mode=pl.Buffered(1))
    return pl.pallas_call(
        functools.partial(_diff_attn_kernel, lam_init=lam_init),
        grid=(DA_HEADS // hg, nk),
        in_specs=[
            _const_spec(lam_vecs.shape),
            _const_spec(g_col.shape),
            col_block((nk, 1, hg * LANES)),
            pl.BlockSpec((nm * LANES, t), lambda g, i: (g, i)),
            col_block((S, hg * LANES)),
            pl.BlockSpec((nk, hg * V_ROWS, t), lambda g, i: (0, g, 0),
                         pipeline_mode=pl.Buffered(1)),
        ],
        out_specs=pl.BlockSpec((t, hg * DA_V), lambda g, i: (i, g)),
        out_shape=jax.ShapeDtypeStruct((S, DA_HEADS * DA_V), BF16),
        scratch_shapes=[pltpu.VMEM((nm, V_ROWS, t), F32), pltpu.VMEM((nm, 1, t), F32),
                        pltpu.VMEM((nm, t, t), BF16), pltpu.VMEM((nm, t, t), BF16),
                        pltpu.VMEM((nm, 1, t), F32), pltpu.VMEM((nm, 1, t), F32),
                        pltpu.VMEM((nm, 1, t), F32), pltpu.VMEM((nm, 1, t), F32)],
        compiler_params=pltpu.CompilerParams(
            dimension_semantics=("arbitrary", "arbitrary"), vmem_limit_bytes=VMEM_LIMIT),
        name="diff_attn",
    )(lam_vecs, g_col, kn, qt, ka, vtblk)


def _sb_attn_kernel(q_ref, k_ref, v_ref, o_ref, acc_ref, carry_ref):
    i = pl.program_id(0)
    t = q_ref.shape[0]
    pairs = q_ref.shape[1] // LANES
    lane = lax.broadcasted_iota(jnp.int32, (t, LANES), 1)
    first = lane < HEAD_DIM
    row = lax.broadcasted_iota(jnp.int32, (t, t), 0)
    col = lax.broadcasted_iota(jnp.int32, (t, t), 1)
    tri = jnp.where(row > col, 1.0, 0.0).astype(BF16)
    strict = jnp.concatenate([col < row, col < row], axis=0)

    def block(j, p, diagonal):
        lanes = slice(p * LANES, (p + 1) * LANES)
        q2 = q_ref[:, lanes]
        zeros = jnp.zeros_like(q2)
        qs = jnp.concatenate([jnp.where(first, q2, zeros), jnp.where(first, zeros, q2)], axis=0)
        rows = pl.ds(pl.multiple_of(j * t, t), t)
        kb = k_ref[rows, lanes]
        vb = v_ref[rows, lanes]
        z = lax.dot_general(qs, kb, (((1,), (1,)), ((), ())),
                            preferred_element_type=F32)
        if diagonal:
            z = jnp.where(strict, z, NEG_BIG)
            carry = jnp.zeros((2 * t, 1), F32)
        else:
            carry = carry_ref[p]
        sp = jnp.where(z > SB_LINEAR_ABOVE, z, jnp.log2(1.0 + jnp.exp2(z)))
        hi = sp.astype(BF16)
        between = jnp.dot(hi, tri, preferred_element_type=F32) + carry
        a = jnp.exp2(z - sp - between)
        av = jnp.dot(a.astype(BF16), vb, preferred_element_type=F32)
        acc_ref[p] = av if diagonal else acc_ref[p] + av
        carry = carry + jnp.sum(sp, axis=1, keepdims=True)
        carry_ref[p] = carry
        return carry

    def all_heads(j, diagonal):
        low = block(j, 0, diagonal)
        for p in range(1, pairs):
            low = jnp.minimum(low, block(j, p, diagonal))
        return jnp.min(low)

    def cond(state):
        return jnp.logical_and(state[0] >= 0, state[1] < SB_ZERO_LOG2)

    def body(state):
        j = state[0]
        return j - 1, all_heads(j, False)

    lax.while_loop(cond, body, (i - 1, all_heads(i, True)))
    for p in range(pairs):
        acc = acc_ref[p]
        o_ref[:, p * LANES:(p + 1) * LANES] = jnp.where(first, acc[:t], acc[t:]).astype(o_ref.dtype)


def _sb_attn(q, k, v, t):
    S, cols = q.shape
    pairs = cols // LANES
    return pl.pallas_call(
        _sb_attn_kernel,
        grid=(S // t,),
        in_specs=[
            pl.BlockSpec((t, cols), lambda i: (i, 0)),
            _const_spec(k.shape),
            _const_spec(v.shape),
        ],
        out_specs=pl.BlockSpec((t, cols), lambda i: (i, 0)),
        out_shape=jax.ShapeDtypeStruct((S, cols), BF16),
        scratch_shapes=[pltpu.VMEM((pairs, 2 * t, LANES), F32), pltpu.VMEM((pairs, 2 * t, 1), F32)],
        compiler_params=pltpu.CompilerParams(
            dimension_semantics=("arbitrary",), vmem_limit_bytes=VMEM_LIMIT),
        name="sb_attn",
    )(q, k, v)


def _mem_kv_kernel(mem_ref, g_ref, w_ref, kv_ref):
    mn = _rms(mem_ref[...], g_ref[...]).astype(BF16)
    kv_ref[...] = jnp.dot(mn, w_ref[...], preferred_element_type=F32).astype(kv_ref.dtype)


def _mem_kv(mem, g, w):
    M = mem.shape[0]
    return pl.pallas_call(
        _mem_kv_kernel,
        out_shape=jax.ShapeDtypeStruct((M, w.shape[1]), BF16),
        compiler_params=pltpu.CompilerParams(vmem_limit_bytes=VMEM_LIMIT),
        name="mem_kv",
    )(mem, g, w)


def _post_mlp_kernel(x_ref, ya_ref, yb_ref, wout_ref, gc_ref, wxq_ref, kv_ref, wxo_ref,
                     gm_ref, wup_ref, wdown_ref, gf_ref, o_ref, *, final_norm):
    D = x_ref.shape[1]
    half = ya_ref.shape[1]
    h = (x_ref[...]
         + jnp.dot(ya_ref[...], wout_ref[:half], preferred_element_type=F32)
         + jnp.dot(yb_ref[...], wout_ref[half:], preferred_element_type=F32))

    hn = _rms(h, gc_ref[...]).astype(BF16)
    xd = D // X_HEADS
    q = jnp.dot(hn, wxq_ref[...], preferred_element_type=F32) * (1.0 / math.sqrt(xd))
    q = q.astype(BF16)
    upd = jnp.zeros_like(h)
    for hd in range(X_HEADS):
        qh = q[:, hd * xd:(hd + 1) * xd]
        kh = kv_ref[:, hd * xd:(hd + 1) * xd]
        vh = kv_ref[:, D + hd * xd:D + (hd + 1) * xd]
        s = lax.dot_general(qh, kh, (((1,), (1,)), ((), ())), preferred_element_type=F32)
        p = jnp.exp(s - jnp.max(s, axis=-1, keepdims=True))
        l = jnp.sum(p, axis=-1, keepdims=True)
        oh = jnp.dot(p.astype(BF16), vh, preferred_element_type=F32) / l
        upd = upd + jnp.dot(oh.astype(BF16), wxo_ref[hd * xd:(hd + 1) * xd, :],
                            preferred_element_type=F32)
    h = h + upd

    hn = _rms(h, gm_ref[...]).astype(BF16)
    upd = jnp.zeros_like(h)
    for c in range(wup_ref.shape[1] // D):
        a = jnp.dot(hn, wup_ref[:, c * D:(c + 1) * D], preferred_element_type=F32)
        a = jnp.square(jnp.maximum(a, 0.0)).astype(BF16)
        upd = upd + jnp.dot(a, wdown_ref[c * D:(c + 1) * D, :], preferred_element_type=F32)
    h = h + upd
    if final_norm:
        h = _rms(h, gf_ref[...])
    o_ref[...] = h


def _post_mlp(x, ya, yb, kv, p, gf, final_norm):
    S, D = x.shape
    ts = min(ROW_TILE, S)
    row = lambda i: (i, 0)
    consts = [p["w_out"], p["g_cross"], p["w_xq"], kv, p["w_xo"],
              p["g_mlp"], p["w_up"], p["w_down"], gf]
    return pl.pallas_call(
        functools.partial(_post_mlp_kernel, final_norm=final_norm),
        grid=(S // ts,),
        in_specs=[pl.BlockSpec((ts, D), row),
                  pl.BlockSpec((ts, ya.shape[1]), row),
                  pl.BlockSpec((ts, yb.shape[1]), row)] + [_const_spec(c.shape) for c in consts],
        out_specs=pl.BlockSpec((ts, D), row),
        out_shape=jax.ShapeDtypeStruct((S, D), F32),
        compiler_params=pltpu.CompilerParams(
            dimension_semantics=("arbitrary",), vmem_limit_bytes=VMEM_LIMIT),
        name="post_mlp",
    )(x, ya, yb, *consts)


def _rope_inv_freq():
    inv = np.float32(ROPE_THETA) ** (-np.arange(0, ROT_DIM, 2, dtype=np.float32) / np.float32(ROT_DIM))
    return jnp.asarray(inv.reshape(ROT_DIM // 2, 1))


def _layer(h, pos_row, invf, mem, p, lam_init, gf, final_norm):
    S, D = h.shape
    td = min(DIFF_TILE, S)
    qt, ka, vtblk, qs, ks, vs, kn = _in_proj(h, pos_row, invf, p["g_mix"], p["w_in"], td)
    ya = _diff_attn(p["lam_vecs"], p["g_subln"], kn, qt, ka, vtblk, lam_init)
    yb = _sb_attn(qs, ks, vs, min(SB_TILE, S))
    kv = _mem_kv(mem, p["g_mem"], p["w_xkv"])
    return _post_mlp(h, ya, yb, kv, p, gf, final_norm)


def kernel(x, mem, positions, g_mix, w_in, lambda_q1, lambda_k1, lambda_q2, lambda_k2, g_subln, w_out, g_cross, g_mem, w_xq, w_xkv, w_xo, g_mlp, w_up, w_down, g_final):
    B, S, D = x.shape
    depth = w_in.shape[0]
    invf = _rope_inv_freq()
    gf = g_final.reshape(1, D).astype(F32)
    outs = []
    for b in range(B):
        h = x[b]
        pos_row = positions[b].reshape(1, S)
        for l in range(depth):
            lam_init = 0.8 - 0.6 * math.exp(-0.3 * l)
            p = {
                "g_mix": g_mix[l].reshape(1, D),
                "w_in": w_in[l].astype(BF16),
                "lam_vecs": jnp.stack([lambda_q1[l], lambda_k1[l], lambda_q2[l], lambda_k2[l]]).astype(F32),
                "g_subln": g_subln[l].reshape(DA_V, 1).astype(F32),
                "w_out": w_out[l].astype(BF16),
                "g_cross": g_cross[l].reshape(1, D),
                "g_mem": g_mem[l].reshape(1, D),
                "w_xq": w_xq[l].astype(BF16),
                "w_xkv": w_xkv[l].astype(BF16),
                "w_xo": w_xo[l].astype(BF16),
                "g_mlp": g_mlp[l].reshape(1, D),
                "w_up": w_up[l].astype(BF16),
                "w_down": w_down[l].astype(BF16),
            }
            h = _layer(h, pos_row, invf, mem[b], p, lam_init, gf, l == depth - 1)
        outs.append(h)
    return jnp.stack(outs)
```

```python
import functools
import math

import numpy as np
import jax
import jax.numpy as jnp
from jax import lax
from jax.experimental import pallas as pl
from jax.experimental.pallas import tpu as pltpu

F32 = jnp.float32
BF16 = jnp.bfloat16

HEAD_DIM = 64
DA_HEADS = 4
DA_V = 2 * HEAD_DIM
SB_HEADS = 8
GROUP_COLS = 512
ROT_DIM = HEAD_DIM // 4
ROPE_THETA = 500000.0
X_HEADS = 4
EPS = 1e-6
LANES = 128
NEG_BIG = -1e30

SB_ZERO_LOG2 = 151.0
SB_LINEAR_ABOVE = 64.0

LAGGED_MAX_SCORE = 45.0
KEY_NORM_SLACK = 1.02

VMEM_LIMIT = 56 * 1024 * 1024

ROW_TILE = 512
DIFF_TILE = 512
DIFF_HEADS_PER_STEP = 4
SB_TILE = 256
SB_BLOCKS_PER_STEP = 2
V_ROWS = DA_V + 16


def _rms(x, g):
    ms = jnp.mean(x * x, axis=-1, keepdims=True)
    return x * lax.rsqrt(ms + EPS) * g


def _const_spec(shape):
    nd = len(shape)
    return pl.BlockSpec(shape, lambda *_: (0,) * nd, pipeline_mode=pl.Buffered(1))


def _in_proj_kernel(x_ref, pos_ref, invf_ref, g_ref, w_ref,
                    qt_ref, ka_ref, vt_ref, qs_ref, ks_ref, vs_ref, kn_ref):
    ts = x_ref.shape[0]
    xn = _rms(x_ref[...], g_ref[...]).astype(BF16)

    half = ROT_DIM // 2
    ang = invf_ref[...] * pos_ref[...].astype(F32)
    cos8 = jnp.cos(ang)
    sin8 = jnp.sin(ang)
    one8 = jnp.ones_like(cos8)
    zero8 = jnp.zeros_like(cos8)
    groups = range(LANES // half)
    in_x1 = [g % (HEAD_DIM // half) == 0 for g in groups]
    in_x2 = [g % (HEAD_DIM // half) == 1 for g in groups]
    cos = jnp.concatenate([cos8 if a or b else one8 for a, b in zip(in_x1, in_x2)], axis=0).T
    s_up = jnp.concatenate([-sin8 if a else zero8 for a in in_x1], axis=0).T
    s_dn = jnp.concatenate([sin8 if b else zero8 for b in in_x2], axis=0).T

    def group(idx):
        w = w_ref[:, idx * GROUP_COLS:(idx + 1) * GROUP_COLS]
        return jnp.dot(xn, w, preferred_element_type=F32)

    def rope(u, scale):
        parts = []
        for c in range(GROUP_COLS // LANES):
            uc = u[:, c * LANES:(c + 1) * LANES]
            r = (uc * cos + pltpu.roll(uc, LANES - ROT_DIM // 2, axis=1) * s_up
                 + pltpu.roll(uc, ROT_DIM // 2, axis=1) * s_dn)
            parts.append(r * scale)
        return jnp.concatenate(parts, axis=1)

    q_scale = math.log2(math.e) / math.sqrt(HEAD_DIM)

    qt = rope(group(0), q_scale).T
    zero_half = jnp.zeros((HEAD_DIM, ts), BF16)
    for hc in range(2 * DA_HEADS):
        c = hc % 2
        piece = qt[hc * HEAD_DIM:(hc + 1) * HEAD_DIM].astype(BF16)
        qt_ref[hc * LANES + c * HEAD_DIM:hc * LANES + (c + 1) * HEAD_DIM, :] = piece
        qt_ref[hc * LANES + (1 - c) * HEAD_DIM:hc * LANES + (2 - c) * HEAD_DIM, :] = zero_half

    ka = rope(group(1), 1.0)
    ka_ref[...] = ka.astype(BF16)
    for h in range(DA_HEADS):
        kh = ka[:, h * LANES:(h + 1) * LANES]
        norm2 = jnp.max(jnp.sum(kh * kh, axis=1, keepdims=True), axis=0, keepdims=True)
        kn_ref[0, :, h * LANES:(h + 1) * LANES] = jnp.broadcast_to(norm2, (1, LANES))

    vt = group(2).astype(BF16).T
    pad_row = lax.broadcasted_iota(jnp.int32, (V_ROWS - DA_V, ts), 0)
    ones_pad = jnp.where(pad_row == 0, 1.0, 0.0).astype(BF16)
    for h in range(DA_HEADS):
        vt_ref[0, h * V_ROWS:h * V_ROWS + DA_V, :] = vt[h * DA_V:(h + 1) * DA_V]
        vt_ref[0, h * V_ROWS + DA_V:(h + 1) * V_ROWS, :] = ones_pad

    qs_ref[...] = (group(3) * q_scale).astype(BF16)
    ks_ref[...] = group(4).astype(BF16)
    vs_ref[...] = group(5).astype(BF16)


def _in_proj(x, pos_row, invf, g, w, ts):
    S, D = x.shape
    nt = S // ts
    flat = jax.ShapeDtypeStruct((S, GROUP_COLS), BF16)
    row = lambda i: (i, 0)
    flat_spec = pl.BlockSpec((ts, GROUP_COLS), row)
    return pl.pallas_call(
        _in_proj_kernel,
        grid=(nt,),
        in_specs=[
            pl.BlockSpec((ts, D), row),
            pl.BlockSpec((1, ts), lambda i: (0, i)),
            _const_spec(invf.shape),
            _const_spec((1, D)),
            _const_spec(w.shape),
        ],
        out_specs=[
            pl.BlockSpec((2 * DA_HEADS * LANES, ts), lambda i: (0, i)),
            flat_spec,
            pl.BlockSpec((1, DA_HEADS * V_ROWS, ts), lambda i: (i, 0, 0)),
            flat_spec, flat_spec, flat_spec,
            pl.BlockSpec((1, 1, DA_HEADS * LANES), lambda i: (i, 0, 0)),
        ],
        out_shape=[
            jax.ShapeDtypeStruct((2 * DA_HEADS * LANES, S), BF16),
            flat,
            jax.ShapeDtypeStruct((nt, DA_HEADS * V_ROWS, ts), BF16),
            flat, flat, flat,
            jax.ShapeDtypeStruct((nt, 1, DA_HEADS * LANES), F32),
        ],
        compiler_params=pltpu.CompilerParams(
            dimension_semantics=("arbitrary",), vmem_limit_bytes=VMEM_LIMIT),
        name="in_proj",
    )(x, pos_row, invf, g, w)


def _diff_attn_kernel(lam_ref, g_ref, kn_ref, qt_ref, k_ref, vt_ref, o_ref,
                      acc_ref, m_ref, pa_ref, pb_ref, ma_ref, mb_ref, fa_ref, fb_ref, *, lam_init):
    i = pl.program_id(1)
    t = qt_ref.shape[1]
    heads = qt_ref.shape[0] // (2 * LANES)
    maps = [(h, 2 * h + c) for h in range(heads) for c in range(2)]
    acc_ref[...] = jnp.zeros_like(acc_ref)
    diag_offset = jnp.minimum(i, 0)

    def scores(j, h, mp, visible_offset):
        kb = k_ref[pl.ds(pl.multiple_of(j * t, t), t), h * LANES:(h + 1) * LANES]
        st = jnp.dot(kb, qt_ref[mp * LANES:(mp + 1) * LANES, :], preferred_element_type=F32)
        if visible_offset is not None:
            key = lax.broadcasted_iota(jnp.int32, st.shape, 0)
            qry = lax.broadcasted_iota(jnp.int32, st.shape, 1)
            st = jnp.where(key - qry <= visible_offset, st, NEG_BIG)
        return st

    def values(j, h):
        return vt_ref[j, h * V_ROWS:(h + 1) * V_ROWS, :]

    def exact_block(j, diagonal):
        for h, mp in maps:
            st = scores(j, h, mp, diag_offset if diagonal else None)
            m_old = m_ref[mp]
            m_new = jnp.maximum(m_old, jnp.max(st, axis=0, keepdims=True))
            alpha = jnp.exp2(m_old - m_new)
            p = jnp.exp2(st - m_new).astype(BF16)
            acc_ref[mp] = alpha * acc_ref[mp] + jnp.dot(values(j, h), p,
                                                        preferred_element_type=F32)
            m_ref[mp] = m_new

    def lagged_weights(j, p_ref, m_in_ref, m_out_ref, f_ref, visible_offset=None):
        for h, mp in maps:
            st = scores(j, h, mp, visible_offset)
            m_old = m_in_ref[mp]
            p_ref[mp] = jnp.exp2(st - m_old).astype(BF16)
            m_new = jnp.maximum(m_old, jnp.max(st, axis=0, keepdims=True))
            f_ref[mp] = jnp.exp2(m_old - m_new)
            m_out_ref[mp] = m_new

    def lagged_accumulate(j, p_ref, f_ref):
        for h, mp in maps:
            acc_ref[mp] = f_ref[mp] * (acc_ref[mp] + jnp.dot(values(j, h), p_ref[mp],
                                                            preferred_element_type=F32))

    tile = lax.broadcasted_iota(jnp.int32, kn_ref.shape, 0)
    kmax2 = jnp.max(jnp.where(tile <= i, kn_ref[...], 0.0)) * KEY_NORM_SLACK
    qn2 = jnp.zeros((1, t), F32)
    for _, mp in maps:
        qm = qt_ref[mp * LANES:(mp + 1) * LANES, :].astype(F32)
        qn2 = jnp.maximum(qn2, jnp.sum(qm * qm, axis=0, keepdims=True))
    safe = jnp.max(qn2) * kmax2 <= LAGGED_MAX_SCORE * LAGGED_MAX_SCORE

    @pl.when(safe)
    def _():
        mb_ref[...] = jnp.zeros_like(mb_ref)

        @pl.when(i == 0)
        def _():
            lagged_weights(0, pa_ref, mb_ref, ma_ref, fa_ref, visible_offset=diag_offset)
            lagged_accumulate(0, pa_ref, fa_ref)

        @pl.when(i > 0)
        def _():
            lagged_weights(0, pa_ref, mb_ref, ma_ref, fa_ref)

            def pair(n, carry):
                j = 2 * n
                lagged_weights(j + 1, pb_ref, ma_ref, mb_ref, fb_ref)
                lagged_accumulate(j, pa_ref, fa_ref)
                lagged_weights(j + 2, pa_ref, mb_ref, ma_ref, fa_ref)
                lagged_accumulate(j + 1, pb_ref, fb_ref)
                return carry

            lax.fori_loop(0, (i - 1) // 2, pair, 0)

            @pl.when(i % 2 == 1)
            def _():
                lagged_weights(i, pb_ref, ma_ref, mb_ref, fb_ref, visible_offset=diag_offset)
                lagged_accumulate(i - 1, pa_ref, fa_ref)
                lagged_accumulate(i, pb_ref, fb_ref)

            @pl.when(i % 2 == 0)
            def _():
                lagged_weights(i - 1, pb_ref, ma_ref, mb_ref, fb_ref)
                lagged_accumulate(i - 2, pa_ref, fa_ref)
                lagged_weights(i, pa_ref, mb_ref, ma_ref, fa_ref, visible_offset=diag_offset)
                lagged_accumulate(i - 1, pb_ref, fb_ref)
                lagged_accumulate(i, pa_ref, fa_ref)

    @pl.when(jnp.logical_not(safe))
    def _():
        m_ref[...] = jnp.full_like(m_ref, NEG_BIG)

        def one(j, carry):
            exact_block(j, False)
            return carry

        lax.fori_loop(0, i, one, 0)
        exact_block(i, True)

    lam_v = lam_ref[...]
    lam = (jnp.exp(jnp.sum(lam_v[0:1] * lam_v[1:2], axis=1, keepdims=True))
           - jnp.exp(jnp.sum(lam_v[2:3] * lam_v[3:4], axis=1, keepdims=True)) + lam_init)
    for h in range(heads):
        a0 = acc_ref[2 * h]
        a1 = acc_ref[2 * h + 1]
        o = a0[:DA_V] / a0[DA_V:DA_V + 1] - lam * (a1[:DA_V] / a1[DA_V:DA_V + 1])
        ms = jnp.mean(o * o, axis=0, keepdims=True)
        y = o * lax.rsqrt(ms + EPS) * g_ref[...] * (1.0 - lam_init)
        o_ref[:, h * DA_V:(h + 1) * DA_V] = y.T.astype(o_ref.dtype)


def _diff_attn(lam_vecs, g_col, kn, qt, ka, vtblk, lam_init):
    S = ka.shape[0]
    nk, _, t = vtblk.shape
    hg = DIFF_HEADS_PER_STEP
    nm = 2 * hg
    col_block = lambda shape: pl.BlockSpec(shape, lambda g, i: (0,) * (len(shape) - 1) + (g,),
                                           pipeline_mode=pl.Buffered(1))
    return pl.pallas_call(
        functools.partial(_diff_attn_kernel, lam_init=lam_init),
        grid=(DA_HEADS // hg, nk),
        in_specs=[
            _const_spec(lam_vecs.shape),
            _const_spec(g_col.shape),
            col_block((nk, 1, hg * LANES)),
            pl.BlockSpec((nm * LANES, t), lambda g, i: (g, i)),
            col_block((S, hg * LANES)),
            pl.BlockSpec((nk, hg * V_ROWS, t), lambda g, i: (0, g, 0),
                         pipeline_mode=pl.Buffered(1)),
        ],
        out_specs=pl.BlockSpec((t, hg * DA_V), lambda g, i: (i, g)),
        out_shape=jax.ShapeDtypeStruct((S, DA_HEADS * DA_V), BF16),
        scratch_shapes=[pltpu.VMEM((nm, V_ROWS, t), F32), pltpu.VMEM((nm, 1, t), F32),
                        pltpu.VMEM((nm, t, t), BF16), pltpu.VMEM((nm, t, t), BF16),
                        pltpu.VMEM((nm, 1, t), F32), pltpu.VMEM((nm, 1, t), F32),
                        pltpu.VMEM((nm, 1, t), F32), pltpu.VMEM((nm, 1, t), F32)],
        compiler_params=pltpu.CompilerParams(
            dimension_semantics=("arbitrary", "arbitrary"), vmem_limit_bytes=VMEM_LIMIT),
        name="diff_attn",
    )(lam_vecs, g_col, kn, qt, ka, vtblk)


def _sb_attn_kernel(q_ref, knew_ref, vnew_ref, o_ref, k_ref, v_ref, acc_ref, carry_ref, *, t):
    i = pl.program_id(0)
    nb = q_ref.shape[0] // t
    pairs = q_ref.shape[1] // LANES
    own = pl.ds(pl.multiple_of(i * nb * t, nb * t), nb * t)
    k_ref[own, :] = knew_ref[...]
    v_ref[own, :] = vnew_ref[...]
    lane = lax.broadcasted_iota(jnp.int32, (t, LANES), 1)
    first = lane < HEAD_DIM
    row = lax.broadcasted_iota(jnp.int32, (t, t), 0)
    col = lax.broadcasted_iota(jnp.int32, (t, t), 1)
    tri = jnp.where(row > col, 1.0, 0.0).astype(BF16)
    strict = jnp.concatenate([col < row, col < row], axis=0)

    def block(j, b, p, diagonal):
        lanes = slice(p * LANES, (p + 1) * LANES)
        slot = b * pairs + p
        q2 = q_ref[b * t:(b + 1) * t, lanes]
        zeros = jnp.zeros_like(q2)
        qs = jnp.concatenate([jnp.where(first, q2, zeros), jnp.where(first, zeros, q2)], axis=0)
        rows = pl.ds(pl.multiple_of(j * t, t), t)
        kb = k_ref[rows, lanes]
        vb = v_ref[rows, lanes]
        z = lax.dot_general(qs, kb, (((1,), (1,)), ((), ())),
                            preferred_element_type=F32)
        if diagonal:
            z = jnp.where(strict, z, NEG_BIG)
            carry = jnp.zeros((2 * t, 1), F32)
        else:
            carry = carry_ref[slot]
        sp = jnp.where(z > SB_LINEAR_ABOVE, z, jnp.log2(1.0 + jnp.exp2(z)))
        hi = sp.astype(BF16)
        between = jnp.dot(hi, tri, preferred_element_type=F32) + carry
        a = jnp.exp2(z - sp - between)
        av = jnp.dot(a.astype(BF16), vb, preferred_element_type=F32)
        acc_ref[slot] = av if diagonal else acc_ref[slot] + av
        carry = carry + jnp.sum(sp, axis=1, keepdims=True)
        carry_ref[slot] = carry
        return carry

    def sweep(back, blocks, diagonal):
        low = None
        for b in blocks:
            for p in range(pairs):
                carry = block(i * nb + b - back, b, p, diagonal)
                low = carry if low is None else jnp.minimum(low, carry)
        return jnp.min(low)

    everyone = range(nb)

    def cond(state):
        return jnp.logical_and(state[0] <= i * nb, state[1] < SB_ZERO_LOG2)

    def body(state):
        return state[0] + 1, sweep(state[0], everyone, False)

    back, _ = lax.while_loop(cond, body, (1, sweep(0, everyone, True)))

    for extra in range(1, nb):
        later = range(extra, nb)
        pending = carry_ref[later[0] * pairs]
        for slot in range(later[0] * pairs + 1, nb * pairs):
            pending = jnp.minimum(pending, carry_ref[slot])

        @pl.when(jnp.logical_and(back == i * nb + extra, jnp.min(pending) < SB_ZERO_LOG2))
        def _():
            sweep(i * nb + extra, later, False)

        back = jnp.where(back == i * nb + extra, back + 1, back)

    for b in range(nb):
        for p in range(pairs):
            acc = acc_ref[b * pairs + p]
            o_ref[b * t:(b + 1) * t, p * LANES:(p + 1) * LANES] = jnp.where(
                first, acc[:t], acc[t:]).astype(o_ref.dtype)


def _sb_attn(q, k, v, t):
    S, cols = q.shape
    pairs = cols // LANES
    nb = min(SB_BLOCKS_PER_STEP, S // t)
    step = pl.BlockSpec((nb * t, cols), lambda i: (i, 0))
    return pl.pallas_call(
        functools.partial(_sb_attn_kernel, t=t),
        grid=(S // (nb * t),),
        in_specs=[step, step, step],
        out_specs=step,
        out_shape=jax.ShapeDtypeStruct((S, cols), BF16),
        scratch_shapes=[pltpu.VMEM((S, cols), BF16), pltpu.VMEM((S, cols), BF16),
                        pltpu.VMEM((nb * pairs, 2 * t, LANES), F32),
                        pltpu.VMEM((nb * pairs, 2 * t, 1), F32)],
        compiler_params=pltpu.CompilerParams(
            dimension_semantics=("arbitrary",), vmem_limit_bytes=VMEM_LIMIT),
        name="sb_attn",
    )(q, k, v)


def _mem_kv_kernel(mem_ref, g_ref, w_ref, kv_ref):
    mn = _rms(mem_ref[...], g_ref[...]).astype(BF16)
    kv_ref[...] = jnp.dot(mn, w_ref[...], preferred_element_type=F32).astype(kv_ref.dtype)


def _mem_kv(mem, g, w):
    M = mem.shape[0]
    return pl.pallas_call(
        _mem_kv_kernel,
        out_shape=jax.ShapeDtypeStruct((M, w.shape[1]), BF16),
        compiler_params=pltpu.CompilerParams(vmem_limit_bytes=VMEM_LIMIT),
        name="mem_kv",
    )(mem, g, w)


def _post_mlp_kernel(x_ref, ya_ref, yb_ref, wout_ref, gc_ref, wxq_ref, kv_ref, wxo_ref,
                     gm_ref, wup_ref, wdown_ref, gf_ref, o_ref, *, final_norm):
    D = x_ref.shape[1]
    half = ya_ref.shape[1]
    h = (x_ref[...]
         + jnp.dot(ya_ref[...], wout_ref[:half], preferred_element_type=F32)
         + jnp.dot(yb_ref[...], wout_ref[half:], preferred_element_type=F32))

    hn = _rms(h, gc_ref[...]).astype(BF16)
    xd = D // X_HEADS
    q = jnp.dot(hn, wxq_ref[...], preferred_element_type=F32) * (1.0 / math.sqrt(xd))
    q = q.astype(BF16)
    upd = jnp.zeros_like(h)
    for hd in range(X_HEADS):
        qh = q[:, hd * xd:(hd + 1) * xd]
        kh = kv_ref[:, hd * xd:(hd + 1) * xd]
        vh = kv_ref[:, D + hd * xd:D + (hd + 1) * xd]
        s = lax.dot_general(qh, kh, (((1,), (1,)), ((), ())), preferred_element_type=F32)
        p = jnp.exp(s - jnp.max(s, axis=-1, keepdims=True))
        l = jnp.sum(p, axis=-1, keepdims=True)
        oh = jnp.dot(p.astype(BF16), vh, preferred_element_type=F32) / l
        upd = upd + jnp.dot(oh.astype(BF16), wxo_ref[hd * xd:(hd + 1) * xd, :],
                            preferred_element_type=F32)
    h = h + upd

    hn = _rms(h, gm_ref[...]).astype(BF16)
    upd = jnp.zeros_like(h)
    for c in range(wup_ref.shape[1] // D):
        a = jnp.dot(hn, wup_ref[:, c * D:(c + 1) * D], preferred_element_type=F32)
        a = jnp.square(jnp.maximum(a, 0.0)).astype(BF16)
        upd = upd + jnp.dot(a, wdown_ref[c * D:(c + 1) * D, :], preferred_element_type=F32)
    h = h + upd
    if final_norm:
        h = _rms(h, gf_ref[...])
    o_ref[...] = h


def _post_mlp(x, ya, yb, kv, p, gf, final_norm):
    S, D = x.shape
    ts = min(ROW_TILE, S)
    row = lambda i: (i, 0)
    consts = [p["w_out"], p["g_cross"], p["w_xq"], kv, p["w_xo"],
              p["g_mlp"], p["w_up"], p["w_down"], gf]
    return pl.pallas_call(
        functools.partial(_post_mlp_kernel, final_norm=final_norm),
        grid=(S // ts,),
        in_specs=[pl.BlockSpec((ts, D), row),
                  pl.BlockSpec((ts, ya.shape[1]), row),
                  pl.BlockSpec((ts, yb.shape[1]), row)] + [_const_spec(c.shape) for c in consts],
        out_specs=pl.BlockSpec((ts, D), row),
        out_shape=jax.ShapeDtypeStruct((S, D), F32),
        compiler_params=pltpu.CompilerParams(
            dimension_semantics=("arbitrary",), vmem_limit_bytes=VMEM_LIMIT),
        name="post_mlp",
    )(x, ya, yb, *consts)


def _rope_inv_freq():
    inv = np.float32(ROPE_THETA) ** (-np.arange(0, ROT_DIM, 2, dtype=np.float32) / np.float32(ROT_DIM))
    return jnp.asarray(inv.reshape(ROT_DIM // 2, 1))


def _layer(h, pos_row, invf, mem, p, lam_init, gf, final_norm):
    S, D = h.shape
    td = min(DIFF_TILE, S)
    qt, ka, vtblk, qs, ks, vs, kn = _in_proj(h, pos_row, invf, p["g_mix"], p["w_in"], td)
    ya = _diff_attn(p["lam_vecs"], p["g_subln"], kn, qt, ka, vtblk, lam_init)
    yb = _sb_attn(qs, ks, vs, min(SB_TILE, S))
    kv = _mem_kv(mem, p["g_mem"], p["w_xkv"])
    return _post_mlp(h, ya, yb, kv, p, gf, final_norm)


def kernel(x, mem, positions, g_mix, w_in, lambda_q1, lambda_k1, lambda_q2, lambda_k2, g_subln, w_out, g_cross, g_mem, w_xq, w_xkv, w_xo, g_mlp, w_up, w_down, g_final):
    B, S, D = x.shape
    depth = w_in.shape[0]
    invf = _rope_inv_freq()
    gf = g_final.reshape(1, D).astype(F32)
    outs = []
    for b in range(B):
        h = x[b]
        pos_row = positions[b].reshape(1, S)
        for l in range(depth):
            lam_init = 0.8 - 0.6 * math.exp(-0.3 * l)
            p = {
                "g_mix": g_mix[l].reshape(1, D),
                "w_in": w_in[l].astype(BF16),
                "lam_vecs": jnp.stack([lambda_q1[l], lambda_k1[l], lambda_q2[l], lambda_k2[l]]).astype(F32),
                "g_subln": g_subln[l].reshape(DA_V, 1).astype(F32),
                "w_out": w_out[l].astype(BF16),
                "g_cross": g_cross[l].reshape(1, D),
                "g_mem": g_mem[l].reshape(1, D),
                "w_xq": w_xq[l].astype(BF16),
                "w_xkv": w_xkv[l].astype(BF16),
                "w_xo": w_xo[l].astype(BF16),
                "g_mlp": g_mlp[l].reshape(1, D),
                "w_up": w_up[l].astype(BF16),
                "w_down": w_down[l].astype(BF16),
            }
            h = _layer(h, pos_row, invf, mem[b], p, lam_init, gf, l == depth - 1)
        outs.append(h)
    return jnp.stack(outs)
```

```python
import functools
import math

import numpy as np
import jax
import jax.numpy as jnp
from jax import lax
from jax.experimental import pallas as pl
from jax.experimental.pallas import tpu as pltpu

F32 = jnp.float32
BF16 = jnp.bfloat16

HEAD_DIM = 64
DA_HEADS = 4
DA_V = 2 * HEAD_DIM
SB_HEADS = 8
GROUP_COLS = 512
ROT_DIM = HEAD_DIM // 4
ROPE_THETA = 500000.0
X_HEADS = 4
EPS = 1e-6
LANES = 128
NEG_BIG = -1e30

SB_ZERO_LOG2 = 151.0
SB_LINEAR_ABOVE = 64.0
SB_PLAIN_BELOW = 100.0

LAGGED_MAX_SCORE = 45.0
KEY_NORM_SLACK = 1.02

VMEM_LIMIT = 56 * 1024 * 1024

ROW_TILE = 512
DIFF_TILE = 512
DIFF_HEADS_PER_STEP = 4
SB_TILE = 256
SB_BLOCKS_PER_STEP = 2
V_ROWS = DA_V + 16


def _rms(x, g):
    ms = jnp.mean(x * x, axis=-1, keepdims=True)
    return x * lax.rsqrt(ms + EPS) * g


def _const_spec(shape):
    nd = len(shape)
    return pl.BlockSpec(shape, lambda *_: (0,) * nd, pipeline_mode=pl.Buffered(1))


def _in_proj_kernel(x_ref, pos_ref, invf_ref, g_ref, w_ref,
                    qt_ref, ka_ref, vt_ref, qs_ref, ks_ref, vs_ref, kn_ref, sn_ref):
    ts = x_ref.shape[0]
    xn = _rms(x_ref[...], g_ref[...]).astype(BF16)

    half = ROT_DIM // 2
    ang = invf_ref[...] * pos_ref[...].astype(F32)
    cos8 = jnp.cos(ang)
    sin8 = jnp.sin(ang)
    one8 = jnp.ones_like(cos8)
    zero8 = jnp.zeros_like(cos8)
    groups = range(LANES // half)
    in_x1 = [g % (HEAD_DIM // half) == 0 for g in groups]
    in_x2 = [g % (HEAD_DIM // half) == 1 for g in groups]
    cos = jnp.concatenate([cos8 if a or b else one8 for a, b in zip(in_x1, in_x2)], axis=0).T
    s_up = jnp.concatenate([-sin8 if a else zero8 for a in in_x1], axis=0).T
    s_dn = jnp.concatenate([sin8 if b else zero8 for b in in_x2], axis=0).T

    def group(idx):
        w = w_ref[:, idx * GROUP_COLS:(idx + 1) * GROUP_COLS]
        return jnp.dot(xn, w, preferred_element_type=F32)

    def rope(u, scale):
        parts = []
        for c in range(GROUP_COLS // LANES):
            uc = u[:, c * LANES:(c + 1) * LANES]
            r = (uc * cos + pltpu.roll(uc, LANES - ROT_DIM // 2, axis=1) * s_up
                 + pltpu.roll(uc, ROT_DIM // 2, axis=1) * s_dn)
            parts.append(r * scale)
        return jnp.concatenate(parts, axis=1)

    q_scale = math.log2(math.e) / math.sqrt(HEAD_DIM)

    qt = rope(group(0), q_scale).T
    zero_half = jnp.zeros((HEAD_DIM, ts), BF16)
    for hc in range(2 * DA_HEADS):
        c = hc % 2
        piece = qt[hc * HEAD_DIM:(hc + 1) * HEAD_DIM].astype(BF16)
        qt_ref[hc * LANES + c * HEAD_DIM:hc * LANES + (c + 1) * HEAD_DIM, :] = piece
        qt_ref[hc * LANES + (1 - c) * HEAD_DIM:hc * LANES + (2 - c) * HEAD_DIM, :] = zero_half

    def put_norms(u, n_ref, base):
        for h in range(GROUP_COLS // LANES):
            uh = u[:, h * LANES:(h + 1) * LANES]
            norm2 = jnp.max(jnp.sum(uh * uh, axis=1, keepdims=True), axis=0, keepdims=True)
            n_ref[0, :, base + h * LANES:base + (h + 1) * LANES] = jnp.broadcast_to(norm2, (1, LANES))

    ka = rope(group(1), 1.0)
    ka_ref[...] = ka.astype(BF16)
    put_norms(ka, kn_ref, 0)

    vt = group(2).astype(BF16).T
    pad_row = lax.broadcasted_iota(jnp.int32, (V_ROWS - DA_V, ts), 0)
    ones_pad = jnp.where(pad_row == 0, 1.0, 0.0).astype(BF16)
    for h in range(DA_HEADS):
        vt_ref[0, h * V_ROWS:h * V_ROWS + DA_V, :] = vt[h * DA_V:(h + 1) * DA_V]
        vt_ref[0, h * V_ROWS + DA_V:(h + 1) * V_ROWS, :] = ones_pad

    qs = group(3) * q_scale
    ks = group(4)
    qs_ref[...] = qs.astype(BF16)
    ks_ref[...] = ks.astype(BF16)
    put_norms(qs, sn_ref, 0)
    put_norms(ks, sn_ref, GROUP_COLS)
    vs_ref[...] = group(5).astype(BF16)


def _in_proj(x, pos_row, invf, g, w, ts):
    S, D = x.shape
    nt = S // ts
    flat = jax.ShapeDtypeStruct((S, GROUP_COLS), BF16)
    row = lambda i: (i, 0)
    flat_spec = pl.BlockSpec((ts, GROUP_COLS), row)
    return pl.pallas_call(
        _in_proj_kernel,
        grid=(nt,),
        in_specs=[
            pl.BlockSpec((ts, D), row),
            pl.BlockSpec((1, ts), lambda i: (0, i)),
            _const_spec(invf.shape),
            _const_spec((1, D)),
            _const_spec(w.shape),
        ],
        out_specs=[
            pl.BlockSpec((2 * DA_HEADS * LANES, ts), lambda i: (0, i)),
            flat_spec,
            pl.BlockSpec((1, DA_HEADS * V_ROWS, ts), lambda i: (i, 0, 0)),
            flat_spec, flat_spec, flat_spec,
            pl.BlockSpec((1, 1, DA_HEADS * LANES), lambda i: (i, 0, 0)),
            pl.BlockSpec((1, 1, 2 * GROUP_COLS), lambda i: (i, 0, 0)),
        ],
        out_shape=[
            jax.ShapeDtypeStruct((2 * DA_HEADS * LANES, S), BF16),
            flat,
            jax.ShapeDtypeStruct((nt, DA_HEADS * V_ROWS, ts), BF16),
            flat, flat, flat,
            jax.ShapeDtypeStruct((nt, 1, DA_HEADS * LANES), F32),
            jax.ShapeDtypeStruct((nt, 1, 2 * GROUP_COLS), F32),
        ],
        compiler_params=pltpu.CompilerParams(
            dimension_semantics=("arbitrary",), vmem_limit_bytes=VMEM_LIMIT),
        name="in_proj",
    )(x, pos_row, invf, g, w)


def _diff_attn_kernel(lam_ref, g_ref, kn_ref, bias_ref, qt_ref, k_ref, vt_ref, o_ref,
                      acc_ref, m_ref, pa_ref, pb_ref, ma_ref, mb_ref, fa_ref, fb_ref, *, lam_init):
    i = pl.program_id(1)
    t = qt_ref.shape[1]
    heads = qt_ref.shape[0] // (2 * LANES)
    maps = [(h, 2 * h + c) for h in range(heads) for c in range(2)]
    acc_ref[...] = jnp.zeros_like(acc_ref)

    def scores(j, h, mp, diagonal):
        kb = k_ref[pl.ds(pl.multiple_of(j * t, t), t), h * LANES:(h + 1) * LANES]
        st = jnp.dot(kb, qt_ref[mp * LANES:(mp + 1) * LANES, :], preferred_element_type=F32)
        if diagonal:
            st = st + bias_ref[...]
        return st

    def values(j, h):
        return vt_ref[j, h * V_ROWS:(h + 1) * V_ROWS, :]

    def exact_block(j, diagonal):
        for h, mp in maps:
            st = scores(j, h, mp, diagonal)
            m_old = m_ref[mp]
            m_new = jnp.maximum(m_old, jnp.max(st, axis=0, keepdims=True))
            alpha = jnp.exp2(m_old - m_new)
            p = jnp.exp2(st - m_new).astype(BF16)
            acc_ref[mp] = alpha * acc_ref[mp] + jnp.dot(values(j, h), p,
                                                        preferred_element_type=F32)
            m_ref[mp] = m_new

    def lagged_weights(j, p_ref, m_in_ref, m_out_ref, f_ref, diagonal=False):
        for h, mp in maps:
            st = scores(j, h, mp, diagonal)
            m_old = m_in_ref[mp]
            p_ref[mp] = jnp.exp2(st - m_old).astype(BF16)
            m_new = jnp.maximum(m_old, jnp.max(st, axis=0, keepdims=True))
            f_ref[mp] = jnp.exp2(m_old - m_new)
            m_out_ref[mp] = m_new

    def lagged_accumulate(j, p_ref, f_ref):
        for h, mp in maps:
            acc_ref[mp] = f_ref[mp] * (acc_ref[mp] + jnp.dot(values(j, h), p_ref[mp],
                                                            preferred_element_type=F32))

    tile = lax.broadcasted_iota(jnp.int32, kn_ref.shape, 0)
    kmax2 = jnp.max(jnp.where(tile <= i, kn_ref[...], 0.0)) * KEY_NORM_SLACK
    qn2 = jnp.zeros((1, t), F32)
    for _, mp in maps:
        qm = qt_ref[mp * LANES:(mp + 1) * LANES, :].astype(F32)
        qn2 = jnp.maximum(qn2, jnp.sum(qm * qm, axis=0, keepdims=True))
    safe = jnp.max(qn2) * kmax2 <= LAGGED_MAX_SCORE * LAGGED_MAX_SCORE

    @pl.when(safe)
    def _():
        mb_ref[...] = jnp.zeros_like(mb_ref)

        @pl.when(i == 0)
        def _():
            lagged_weights(0, pa_ref, mb_ref, ma_ref, fa_ref, diagonal=True)
            lagged_accumulate(0, pa_ref, fa_ref)

        @pl.when(i > 0)
        def _():
            lagged_weights(0, pa_ref, mb_ref, ma_ref, fa_ref)

            def pair(n, carry):
                j = 2 * n
                lagged_weights(j + 1, pb_ref, ma_ref, mb_ref, fb_ref)
                lagged_accumulate(j, pa_ref, fa_ref)
                lagged_weights(j + 2, pa_ref, mb_ref, ma_ref, fa_ref)
                lagged_accumulate(j + 1, pb_ref, fb_ref)
                return carry

            lax.fori_loop(0, (i - 1) // 2, pair, 0)

            @pl.when(i % 2 == 1)
            def _():
                lagged_weights(i, pb_ref, ma_ref, mb_ref, fb_ref, diagonal=True)
                lagged_accumulate(i - 1, pa_ref, fa_ref)
                lagged_accumulate(i, pb_ref, fb_ref)

            @pl.when(i % 2 == 0)
            def _():
                lagged_weights(i - 1, pb_ref, ma_ref, mb_ref, fb_ref)
                lagged_accumulate(i - 2, pa_ref, fa_ref)
                lagged_weights(i, pa_ref, mb_ref, ma_ref, fa_ref, diagonal=True)
                lagged_accumulate(i - 1, pb_ref, fb_ref)
                lagged_accumulate(i, pa_ref, fa_ref)

    @pl.when(jnp.logical_not(safe))
    def _():
        m_ref[...] = jnp.full_like(m_ref, NEG_BIG)

        def one(j, carry):
            exact_block(j, False)
            return carry

        lax.fori_loop(0, i, one, 0)
        exact_block(i, True)

    lam_v = lam_ref[...]
    lam = (jnp.exp(jnp.sum(lam_v[0:1] * lam_v[1:2], axis=1, keepdims=True))
           - jnp.exp(jnp.sum(lam_v[2:3] * lam_v[3:4], axis=1, keepdims=True)) + lam_init)
    for h in range(heads):
        a0 = acc_ref[2 * h]
        a1 = acc_ref[2 * h + 1]
        o = a0[:DA_V] / a0[DA_V:DA_V + 1] - lam * (a1[:DA_V] / a1[DA_V:DA_V + 1])
        ms = jnp.mean(o * o, axis=0, keepdims=True)
        y = o * lax.rsqrt(ms + EPS) * g_ref[...] * (1.0 - lam_init)
        o_ref[:, h * DA_V:(h + 1) * DA_V] = y.T.astype(o_ref.dtype)


def _diff_attn(lam_vecs, g_col, kn, qt, ka, vtblk, lam_init):
    S = ka.shape[0]
    nk, _, t = vtblk.shape
    key, qry = np.arange(t)[:, None], np.arange(t)[None, :]
    causal_bias = jnp.asarray(np.where(key <= qry, 0.0, NEG_BIG), F32)
    hg = DIFF_HEADS_PER_STEP
    nm = 2 * hg
    col_block = lambda shape: pl.BlockSpec(shape, lambda g, i: (0,) * (len(shape) - 1) + (g,),
                                           pipeline_mode=pl.Buffered(1))
    return pl.pallas_call(
        functools.partial(_diff_attn_kernel, lam_init=lam_init),
        grid=(DA_HEADS // hg, nk),
        in_specs=[
            _const_spec(lam_vecs.shape),
            _const_spec(g_col.shape),
            col_block((nk, 1, hg * LANES)),
            _const_spec((t, t)),
            pl.BlockSpec((nm * LANES, t), lambda g, i: (g, i)),
            col_block((S, hg * LANES)),
            pl.BlockSpec((nk, hg * V_ROWS, t), lambda g, i: (0, g, 0),
                         pipeline_mode=pl.Buffered(1)),
        ],
        out_specs=pl.BlockSpec((t, hg * DA_V), lambda g, i: (i, g)),
        out_shape=jax.ShapeDtypeStruct((S, DA_HEADS * DA_V), BF16),
        scratch_shapes=[pltpu.VMEM((nm, V_ROWS, t), F32), pltpu.VMEM((nm, 1, t), F32),
                        pltpu.VMEM((nm, t, t), BF16), pltpu.VMEM((nm, t, t), BF16),
                        pltpu.VMEM((nm, 1, t), F32), pltpu.VMEM((nm, 1, t), F32),
                        pltpu.VMEM((nm, 1, t), F32), pltpu.VMEM((nm, 1, t), F32)],
        compiler_params=pltpu.CompilerParams(
            dimension_semantics=("arbitrary", "arbitrary"), vmem_limit_bytes=VMEM_LIMIT),
        name="diff_attn",
    )(lam_vecs, g_col, kn, causal_bias, qt, ka, vtblk)


def _sb_attn_kernel(sn_ref, q_ref, knew_ref, vnew_ref, o_ref, k_ref, v_ref, acc_ref, carry_ref,
                    *, t):
    i = pl.program_id(0)
    nb = q_ref.shape[0] // t
    pairs = q_ref.shape[1] // LANES
    own = pl.ds(pl.multiple_of(i * nb * t, nb * t), nb * t)
    k_ref[own, :] = knew_ref[...]
    v_ref[own, :] = vnew_ref[...]
    lane = lax.broadcasted_iota(jnp.int32, (t, LANES), 1)
    first = lane < HEAD_DIM
    row = lax.broadcasted_iota(jnp.int32, (t, t), 0)
    col = lax.broadcasted_iota(jnp.int32, (t, t), 1)
    tri = jnp.where(row > col, 1.0, 0.0).astype(BF16)
    strict = jnp.concatenate([col < row, col < row], axis=0)

    def block(j, b, p, diagonal, plain):
        lanes = slice(p * LANES, (p + 1) * LANES)
        slot = b * pairs + p
        q2 = q_ref[b * t:(b + 1) * t, lanes]
        zeros = jnp.zeros_like(q2)
        qs = jnp.concatenate([jnp.where(first, q2, zeros), jnp.where(first, zeros, q2)], axis=0)
        rows = pl.ds(pl.multiple_of(j * t, t), t)
        kb = k_ref[rows, lanes]
        vb = v_ref[rows, lanes]
        z = lax.dot_general(qs, kb, (((1,), (1,)), ((), ())),
                            preferred_element_type=F32)
        if diagonal:
            z = jnp.where(strict, z, NEG_BIG)
            carry = jnp.zeros((2 * t, 1), F32)
        else:
            carry = carry_ref[slot]
        sp = jnp.log2(1.0 + jnp.exp2(z))
        if not plain:
            sp = jnp.where(z > SB_LINEAR_ABOVE, z, sp)
        hi = sp.astype(BF16)
        between = jnp.dot(hi, tri, preferred_element_type=F32) + carry
        a = jnp.exp2(z - sp - between)
        av = jnp.dot(a.astype(BF16), vb, preferred_element_type=F32)
        acc_ref[slot] = av if diagonal else acc_ref[slot] + av
        carry = carry + jnp.sum(sp, axis=1, keepdims=True)
        carry_ref[slot] = carry
        return carry

    def run(plain):
        def sweep(back, blocks, diagonal):
            low = None
            for b in blocks:
                for p in range(pairs):
                    carry = block(i * nb + b - back, b, p, diagonal, plain)
                    low = carry if low is None else jnp.minimum(low, carry)
            return jnp.min(low)

        everyone = range(nb)

        def cond(state):
            return jnp.logical_and(state[0] <= i * nb, state[1] < SB_ZERO_LOG2)

        def body(state):
            return state[0] + 1, sweep(state[0], everyone, False)

        back, _ = lax.while_loop(cond, body, (1, sweep(0, everyone, True)))

        for extra in range(1, nb):
            later = range(extra, nb)
            pending = carry_ref[later[0] * pairs]
            for slot in range(later[0] * pairs + 1, nb * pairs):
                pending = jnp.minimum(pending, carry_ref[slot])

            @pl.when(jnp.logical_and(back == i * nb + extra, jnp.min(pending) < SB_ZERO_LOG2))
            def _():
                sweep(i * nb + extra, later, False)

            back = jnp.where(back == i * nb + extra, back + 1, back)

    cols = q_ref.shape[1]
    bound2 = (jnp.max(sn_ref[:, :, :cols]) * jnp.max(sn_ref[:, :, cols:])
              * (KEY_NORM_SLACK * KEY_NORM_SLACK))
    plain_ok = bound2 <= SB_PLAIN_BELOW * SB_PLAIN_BELOW

    @pl.when(plain_ok)
    def _():
        run(True)

    @pl.when(jnp.logical_not(plain_ok))
    def _():
        run(False)

    for b in range(nb):
        for p in range(pairs):
            acc = acc_ref[b * pairs + p]
            o_ref[b * t:(b + 1) * t, p * LANES:(p + 1) * LANES] = jnp.where(
                first, acc[:t], acc[t:]).astype(o_ref.dtype)


def _sb_attn(sn, q, k, v, t):
    S, cols = q.shape
    pairs = cols // LANES
    nb = min(SB_BLOCKS_PER_STEP, S // t)
    step = pl.BlockSpec((nb * t, cols), lambda i: (i, 0))
    return pl.pallas_call(
        functools.partial(_sb_attn_kernel, t=t),
        grid=(S // (nb * t),),
        in_specs=[_const_spec(sn.shape), step, step, step],
        out_specs=step,
        out_shape=jax.ShapeDtypeStruct((S, cols), BF16),
        scratch_shapes=[pltpu.VMEM((S, cols), BF16), pltpu.VMEM((S, cols), BF16),
                        pltpu.VMEM((nb * pairs, 2 * t, LANES), F32),
                        pltpu.VMEM((nb * pairs, 2 * t, 1), F32)],
        compiler_params=pltpu.CompilerParams(
            dimension_semantics=("arbitrary",), vmem_limit_bytes=VMEM_LIMIT),
        name="sb_attn",
    )(sn, q, k, v)


def _mem_kv_kernel(mem_ref, g_ref, w_ref, kv_ref):
    mn = _rms(mem_ref[...], g_ref[...]).astype(BF16)
    kv_ref[...] = jnp.dot(mn, w_ref[...], preferred_element_type=F32).astype(kv_ref.dtype)


def _mem_kv(mem, g, w):
    M = mem.shape[0]
    return pl.pallas_call(
        _mem_kv_kernel,
        out_shape=jax.ShapeDtypeStruct((M, w.shape[1]), BF16),
        compiler_params=pltpu.CompilerParams(vmem_limit_bytes=VMEM_LIMIT),
        name="mem_kv",
    )(mem, g, w)


def _post_mlp_kernel(x_ref, ya_ref, yb_ref, wout_ref, gc_ref, wxq_ref, kv_ref, wxo_ref,
                     gm_ref, wup_ref, wdown_ref, gf_ref, o_ref, *, final_norm):
    D = x_ref.shape[1]
    half = ya_ref.shape[1]
    h = (x_ref[...]
         + jnp.dot(ya_ref[...], wout_ref[:half], preferred_element_type=F32)
         + jnp.dot(yb_ref[...], wout_ref[half:], preferred_element_type=F32))

    hn = _rms(h, gc_ref[...]).astype(BF16)
    xd = D // X_HEADS
    q = jnp.dot(hn, wxq_ref[...], preferred_element_type=F32) * (1.0 / math.sqrt(xd))
    q = q.astype(BF16)
    upd = jnp.zeros_like(h)
    for hd in range(X_HEADS):
        qh = q[:, hd * xd:(hd + 1) * xd]
        kh = kv_ref[:, hd * xd:(hd + 1) * xd]
        vh = kv_ref[:, D + hd * xd:D + (hd + 1) * xd]
        s = lax.dot_general(qh, kh, (((1,), (1,)), ((), ())), preferred_element_type=F32)
        p = jnp.exp(s - jnp.max(s, axis=-1, keepdims=True))
        l = jnp.sum(p, axis=-1, keepdims=True)
        oh = jnp.dot(p.astype(BF16), vh, preferred_element_type=F32) / l
        upd = upd + jnp.dot(oh.astype(BF16), wxo_ref[hd * xd:(hd + 1) * xd, :],
                            preferred_element_type=F32)
    h = h + upd

    hn = _rms(h, gm_ref[...]).astype(BF16)
    upd = jnp.zeros_like(h)
    for c in range(wup_ref.shape[1] // D):
        a = jnp.dot(hn, wup_ref[:, c * D:(c + 1) * D], preferred_element_type=F32)
        a = jnp.square(jnp.maximum(a, 0.0)).astype(BF16)
        upd = upd + jnp.dot(a, wdown_ref[c * D:(c + 1) * D, :], preferred_element_type=F32)
    h = h + upd
    if final_norm:
        h = _rms(h, gf_ref[...])
    o_ref[...] = h


def _post_mlp(x, ya, yb, kv, p, gf, final_norm):
    S, D = x.shape
    ts = min(ROW_TILE, S)
    row = lambda i: (i, 0)
    consts = [p["w_out"], p["g_cross"], p["w_xq"], kv, p["w_xo"],
              p["g_mlp"], p["w_up"], p["w_down"], gf]
    return pl.pallas_call(
        functools.partial(_post_mlp_kernel, final_norm=final_norm),
        grid=(S // ts,),
        in_specs=[pl.BlockSpec((ts, D), row),
                  pl.BlockSpec((ts, ya.shape[1]), row),
                  pl.BlockSpec((ts, yb.shape[1]), row)] + [_const_spec(c.shape) for c in consts],
        out_specs=pl.BlockSpec((ts, D), row),
        out_shape=jax.ShapeDtypeStruct((S, D), F32),
        compiler_params=pltpu.CompilerParams(
            dimension_semantics=("arbitrary",), vmem_limit_bytes=VMEM_LIMIT),
        name="post_mlp",
    )(x, ya, yb, *consts)


def _rope_inv_freq():
    inv = np.float32(ROPE_THETA) ** (-np.arange(0, ROT_DIM, 2, dtype=np.float32) / np.float32(ROT_DIM))
    return jnp.asarray(inv.reshape(ROT_DIM // 2, 1))


def _layer(h, pos_row, invf, mem, p, lam_init, gf, final_norm):
    S, D = h.shape
    td = min(DIFF_TILE, S)
    assert S % td == 0 and S % min(SB_TILE * SB_BLOCKS_PER_STEP, S) == 0 and S % min(ROW_TILE, S) == 0, S
    qt, ka, vtblk, qs, ks, vs, kn, sn = _in_proj(h, pos_row, invf, p["g_mix"], p["w_in"], td)
    ya = _diff_attn(p["lam_vecs"], p["g_subln"], kn, qt, ka, vtblk, lam_init)
    yb = _sb_attn(sn, qs, ks, vs, min(SB_TILE, S))
    kv = _mem_kv(mem, p["g_mem"], p["w_xkv"])
    return _post_mlp(h, ya, yb, kv, p, gf, final_norm)


def kernel(x, mem, positions, g_mix, w_in, lambda_q1, lambda_k1, lambda_q2, lambda_k2, g_subln, w_out, g_cross, g_mem, w_xq, w_xkv, w_xo, g_mlp, w_up, w_down, g_final):
    B, S, D = x.shape
    depth = w_in.shape[0]
    invf = _rope_inv_freq()
    gf = g_final.reshape(1, D).astype(F32)
    outs = []
    for b in range(B):
        h = x[b]
        pos_row = positions[b].reshape(1, S)
        for l in range(depth):
            lam_init = 0.8 - 0.6 * math.exp(-0.3 * l)
            p = {
                "g_mix": g_mix[l].reshape(1, D),
                "w_in": w_in[l].astype(BF16),
                "lam_vecs": jnp.stack([lambda_q1[l], lambda_k1[l], lambda_q2[l], lambda_k2[l]]).astype(F32),
                "g_subln": g_subln[l].reshape(DA_V, 1).astype(F32),
                "w_out": w_out[l].astype(BF16),
                "g_cross": g_cross[l].reshape(1, D),
                "g_mem": g_mem[l].reshape(1, D),
                "w_xq": w_xq[l].astype(BF16),
                "w_xkv": w_xkv[l].astype(BF16),
                "w_xo": w_xo[l].astype(BF16),
                "g_mlp": g_mlp[l].reshape(1, D),
                "w_up": w_up[l].astype(BF16),
                "w_down": w_down[l].astype(BF16),
            }
            h = _layer(h, pos_row, invf, mem[b], p, lam_init, gf, l == depth - 1)
        outs.append(h)
    return jnp.stack(outs)
```

```python
import functools
import math

import numpy as np
import jax
import jax.numpy as jnp
from jax import lax
from jax.experimental import pallas as pl
from jax.experimental.pallas import tpu as pltpu

F32 = jnp.float32
BF16 = jnp.bfloat16

HEAD_DIM = 64
DA_HEADS = 4
DA_V = 2 * HEAD_DIM
SB_HEADS = 8
GROUP_COLS = 512
ROT_DIM = HEAD_DIM // 4
ROPE_THETA = 500000.0
X_HEADS = 4
EPS = 1e-6
LANES = 128
NEG_BIG = -1e30

SB_ZERO_LOG2 = 151.0
SB_LINEAR_ABOVE = 64.0

LAGGED_MAX_SCORE = 45.0
KEY_NORM_SLACK = 1.02

VMEM_LIMIT = 56 * 1024 * 1024

ROW_TILE = 512
DIFF_TILE = 512
DIFF_HEADS_PER_STEP = 4
SB_TILE = 256
SB_BLOCKS_PER_STEP = 2
V_ROWS = DA_V + 16


def _rms(x, g):
    ms = jnp.mean(x * x, axis=-1, keepdims=True)
    return x * lax.rsqrt(ms + EPS) * g


def _const_spec(shape):
    nd = len(shape)
    return pl.BlockSpec(shape, lambda *_: (0,) * nd, pipeline_mode=pl.Buffered(1))


def _in_proj_kernel(x_ref, pos_ref, invf_ref, g_ref, w_ref,
                    qt_ref, ka_ref, vt_ref, qs_ref, ks_ref, vs_ref, kn_ref):
    ts = x_ref.shape[0]
    xn = _rms(x_ref[...], g_ref[...]).astype(BF16)

    half = ROT_DIM // 2
    ang = invf_ref[...] * pos_ref[...].astype(F32)
    cos8 = jnp.cos(ang)
    sin8 = jnp.sin(ang)
    one8 = jnp.ones_like(cos8)
    zero8 = jnp.zeros_like(cos8)
    groups = range(LANES // half)
    in_x1 = [g % (HEAD_DIM // half) == 0 for g in groups]
    in_x2 = [g % (HEAD_DIM // half) == 1 for g in groups]
    cos = jnp.concatenate([cos8 if a or b else one8 for a, b in zip(in_x1, in_x2)], axis=0).T
    s_up = jnp.concatenate([-sin8 if a else zero8 for a in in_x1], axis=0).T
    s_dn = jnp.concatenate([sin8 if b else zero8 for b in in_x2], axis=0).T

    def group(idx):
        w = w_ref[:, idx * GROUP_COLS:(idx + 1) * GROUP_COLS]
        return jnp.dot(xn, w, preferred_element_type=F32)

    def rope(u, scale):
        parts = []
        for c in range(GROUP_COLS // LANES):
            uc = u[:, c * LANES:(c + 1) * LANES]
            r = (uc * cos + pltpu.roll(uc, LANES - ROT_DIM // 2, axis=1) * s_up
                 + pltpu.roll(uc, ROT_DIM // 2, axis=1) * s_dn)
            parts.append(r * scale)
        return jnp.concatenate(parts, axis=1)

    q_scale = math.log2(math.e) / math.sqrt(HEAD_DIM)

    qt = rope(group(0), q_scale).T
    zero_half = jnp.zeros((HEAD_DIM, ts), BF16)
    for hc in range(2 * DA_HEADS):
        c = hc % 2
        piece = qt[hc * HEAD_DIM:(hc + 1) * HEAD_DIM].astype(BF16)
        qt_ref[hc * LANES + c * HEAD_DIM:hc * LANES + (c + 1) * HEAD_DIM, :] = piece
        qt_ref[hc * LANES + (1 - c) * HEAD_DIM:hc * LANES + (2 - c) * HEAD_DIM, :] = zero_half

    ka = rope(group(1), 1.0)
    ka_ref[...] = ka.astype(BF16)
    for h in range(DA_HEADS):
        kh = ka[:, h * LANES:(h + 1) * LANES]
        norm2 = jnp.max(jnp.sum(kh * kh, axis=1, keepdims=True), axis=0, keepdims=True)
        kn_ref[0, :, h * LANES:(h + 1) * LANES] = jnp.broadcast_to(norm2, (1, LANES))

    vt = group(2).astype(BF16).T
    pad_row = lax.broadcasted_iota(jnp.int32, (V_ROWS - DA_V, ts), 0)
    ones_pad = jnp.where(pad_row == 0, 1.0, 0.0).astype(BF16)
    for h in range(DA_HEADS):
        vt_ref[0, h * V_ROWS:h * V_ROWS + DA_V, :] = vt[h * DA_V:(h + 1) * DA_V]
        vt_ref[0, h * V_ROWS + DA_V:(h + 1) * V_ROWS, :] = ones_pad

    qs_ref[...] = (group(3) * q_scale).astype(BF16)
    ks_ref[...] = group(4).astype(BF16)
    vs_ref[...] = group(5).astype(BF16)


def _in_proj(x, pos_row, invf, g, w, ts):
    S, D = x.shape
    nt = S // ts
    flat = jax.ShapeDtypeStruct((S, GROUP_COLS), BF16)
    row = lambda i: (i, 0)
    flat_spec = pl.BlockSpec((ts, GROUP_COLS), row)
    return pl.pallas_call(
        _in_proj_kernel,
        grid=(nt,),
        in_specs=[
            pl.BlockSpec((ts, D), row),
            pl.BlockSpec((1, ts), lambda i: (0, i)),
            _const_spec(invf.shape),
            _const_spec((1, D)),
            _const_spec(w.shape),
        ],
        out_specs=[
            pl.BlockSpec((2 * DA_HEADS * LANES, ts), lambda i: (0, i)),
            flat_spec,
            pl.BlockSpec((1, DA_HEADS * V_ROWS, ts), lambda i: (i, 0, 0)),
            flat_spec, flat_spec, flat_spec,
            pl.BlockSpec((1, 1, DA_HEADS * LANES), lambda i: (i, 0, 0)),
        ],
        out_shape=[
            jax.ShapeDtypeStruct((2 * DA_HEADS * LANES, S), BF16),
            flat,
            jax.ShapeDtypeStruct((nt, DA_HEADS * V_ROWS, ts), BF16),
            flat, flat, flat,
            jax.ShapeDtypeStruct((nt, 1, DA_HEADS * LANES), F32),
        ],
        compiler_params=pltpu.CompilerParams(
            dimension_semantics=("arbitrary",), vmem_limit_bytes=VMEM_LIMIT),
        name="in_proj",
    )(x, pos_row, invf, g, w)


def _diff_attn_kernel(lam_ref, g_ref, kn_ref, bias_ref, qt_ref, k_ref, vt_ref, o_ref,
                      acc_ref, m_ref, pa_ref, pb_ref, ma_ref, mb_ref, fa_ref, fb_ref, *, lam_init):
    i = pl.program_id(1)
    t = qt_ref.shape[1]
    heads = qt_ref.shape[0] // (2 * LANES)
    maps = [(h, 2 * h + c) for h in range(heads) for c in range(2)]
    acc_ref[...] = jnp.zeros_like(acc_ref)

    def scores(j, h, mp, diagonal):
        kb = k_ref[pl.ds(pl.multiple_of(j * t, t), t), h * LANES:(h + 1) * LANES]
        st = jnp.dot(kb, qt_ref[mp * LANES:(mp + 1) * LANES, :], preferred_element_type=F32)
        if diagonal:
            st = st + bias_ref[...]
        return st

    def values(j, h):
        return vt_ref[j, h * V_ROWS:(h + 1) * V_ROWS, :]

    def exact_block(j, diagonal):
        for h, mp in maps:
            st = scores(j, h, mp, diagonal)
            m_old = m_ref[mp]
            m_new = jnp.maximum(m_old, jnp.max(st, axis=0, keepdims=True))
            alpha = jnp.exp2(m_old - m_new)
            p = jnp.exp2(st - m_new).astype(BF16)
            acc_ref[mp] = alpha * acc_ref[mp] + jnp.dot(values(j, h), p,
                                                        preferred_element_type=F32)
            m_ref[mp] = m_new

    def lagged_weights(j, p_ref, m_in_ref, m_out_ref, f_ref, diagonal=False):
        for h, mp in maps:
            st = scores(j, h, mp, diagonal)
            m_old = m_in_ref[mp]
            p_ref[mp] = jnp.exp2(st - m_old).astype(BF16)
            m_new = jnp.maximum(m_old, jnp.max(st, axis=0, keepdims=True))
            f_ref[mp] = jnp.exp2(m_old - m_new)
            m_out_ref[mp] = m_new

    def lagged_accumulate(j, p_ref, f_ref):
        for h, mp in maps:
            acc_ref[mp] = f_ref[mp] * (acc_ref[mp] + jnp.dot(values(j, h), p_ref[mp],
                                                            preferred_element_type=F32))

    tile = lax.broadcasted_iota(jnp.int32, kn_ref.shape, 0)
    kmax2 = jnp.max(jnp.where(tile <= i, kn_ref[...], 0.0)) * KEY_NORM_SLACK
    qn2 = jnp.zeros((1, t), F32)
    for _, mp in maps:
        qm = qt_ref[mp * LANES:(mp + 1) * LANES, :].astype(F32)
        qn2 = jnp.maximum(qn2, jnp.sum(qm * qm, axis=0, keepdims=True))
    safe = jnp.max(qn2) * kmax2 <= LAGGED_MAX_SCORE * LAGGED_MAX_SCORE

    @pl.when(safe)
    def _():
        mb_ref[...] = jnp.zeros_like(mb_ref)

        @pl.when(i == 0)
        def _():
            lagged_weights(0, pa_ref, mb_ref, ma_ref, fa_ref, diagonal=True)
            lagged_accumulate(0, pa_ref, fa_ref)

        @pl.when(i > 0)
        def _():
            lagged_weights(0, pa_ref, mb_ref, ma_ref, fa_ref)

            def pair(n, carry):
                j = 2 * n
                lagged_weights(j + 1, pb_ref, ma_ref, mb_ref, fb_ref)
                lagged_accumulate(j, pa_ref, fa_ref)
                lagged_weights(j + 2, pa_ref, mb_ref, ma_ref, fa_ref)
                lagged_accumulate(j + 1, pb_ref, fb_ref)
                return carry

            lax.fori_loop(0, (i - 1) // 2, pair, 0)

            @pl.when(i % 2 == 1)
            def _():
                lagged_weights(i, pb_ref, ma_ref, mb_ref, fb_ref, diagonal=True)
                lagged_accumulate(i - 1, pa_ref, fa_ref)
                lagged_accumulate(i, pb_ref, fb_ref)

            @pl.when(i % 2 == 0)
            def _():
                lagged_weights(i - 1, pb_ref, ma_ref, mb_ref, fb_ref)
                lagged_accumulate(i - 2, pa_ref, fa_ref)
                lagged_weights(i, pa_ref, mb_ref, ma_ref, fa_ref, diagonal=True)
                lagged_accumulate(i - 1, pb_ref, fb_ref)
                lagged_accumulate(i, pa_ref, fa_ref)

    @pl.when(jnp.logical_not(safe))
    def _():
        m_ref[...] = jnp.full_like(m_ref, NEG_BIG)

        def one(j, carry):
            exact_block(j, False)
            return carry

        lax.fori_loop(0, i, one, 0)
        exact_block(i, True)

    lam_v = lam_ref[...]
    lam = (jnp.exp(jnp.sum(lam_v[0:1] * lam_v[1:2], axis=1, keepdims=True))
           - jnp.exp(jnp.sum(lam_v[2:3] * lam_v[3:4], axis=1, keepdims=True)) + lam_init)
    for h in range(heads):
        a0 = acc_ref[2 * h]
        a1 = acc_ref[2 * h + 1]
        o = a0[:DA_V] / a0[DA_V:DA_V + 1] - lam * (a1[:DA_V] / a1[DA_V:DA_V + 1])
        ms = jnp.mean(o * o, axis=0, keepdims=True)
        y = o * lax.rsqrt(ms + EPS) * g_ref[...] * (1.0 - lam_init)
        o_ref[:, h * DA_V:(h + 1) * DA_V] = y.T.astype(o_ref.dtype)


def _diff_attn(lam_vecs, g_col, kn, qt, ka, vtblk, lam_init):
    S = ka.shape[0]
    nk, _, t = vtblk.shape
    key, qry = np.arange(t)[:, None], np.arange(t)[None, :]
    causal_bias = jnp.asarray(np.where(key <= qry, 0.0, NEG_BIG), F32)
    hg = DIFF_HEADS_PER_STEP
    nm = 2 * hg
    col_block = lambda shape: pl.BlockSpec(shape, lambda g, i: (0,) * (len(shape) - 1) + (g,),
                                           pipeline_mode=pl.Buffered(1))
    return pl.pallas_call(
        functools.partial(_diff_attn_kernel, lam_init=lam_init),
        grid=(DA_HEADS // hg, nk),
        in_specs=[
            _const_spec(lam_vecs.shape),
            _const_spec(g_col.shape),
            col_block((nk, 1, hg * LANES)),
            _const_spec((t, t)),
            pl.BlockSpec((nm * LANES, t), lambda g, i: (g, i)),
            col_block((S, hg * LANES)),
            pl.BlockSpec((nk, hg * V_ROWS, t), lambda g, i: (0, g, 0),
                         pipeline_mode=pl.Buffered(1)),
        ],
        out_specs=pl.BlockSpec((t, hg * DA_V), lambda g, i: (i, g)),
        out_shape=jax.ShapeDtypeStruct((S, DA_HEADS * DA_V), BF16),
        scratch_shapes=[pltpu.VMEM((nm, V_ROWS, t), F32), pltpu.VMEM((nm, 1, t), F32),
                        pltpu.VMEM((nm, t, t), BF16), pltpu.VMEM((nm, t, t), BF16),
                        pltpu.VMEM((nm, 1, t), F32), pltpu.VMEM((nm, 1, t), F32),
                        pltpu.VMEM((nm, 1, t), F32), pltpu.VMEM((nm, 1, t), F32)],
        compiler_params=pltpu.CompilerParams(
            dimension_semantics=("arbitrary", "arbitrary"), vmem_limit_bytes=VMEM_LIMIT),
        name="diff_attn",
    )(lam_vecs, g_col, kn, causal_bias, qt, ka, vtblk)


def _sb_attn_kernel(q_ref, knew_ref, vnew_ref, o_ref, k_ref, v_ref, acc_ref, carry_ref, *, t):
    i = pl.program_id(0)
    nb = q_ref.shape[0] // t
    pairs = q_ref.shape[1] // LANES
    own = pl.ds(pl.multiple_of(i * nb * t, nb * t), nb * t)
    k_ref[own, :] = knew_ref[...]
    v_ref[own, :] = vnew_ref[...]
    lane = lax.broadcasted_iota(jnp.int32, (t, LANES), 1)
    first = lane < HEAD_DIM
    row = lax.broadcasted_iota(jnp.int32, (t, t), 0)
    col = lax.broadcasted_iota(jnp.int32, (t, t), 1)
    tri = jnp.where(row > col, 1.0, 0.0).astype(BF16)
    strict = jnp.concatenate([col < row, col < row], axis=0)

    def block(j, b, p, diagonal):
        lanes = slice(p * LANES, (p + 1) * LANES)
        slot = b * pairs + p
        q2 = q_ref[b * t:(b + 1) * t, lanes]
        zeros = jnp.zeros_like(q2)
        qs = jnp.concatenate([jnp.where(first, q2, zeros), jnp.where(first, zeros, q2)], axis=0)
        rows = pl.ds(pl.multiple_of(j * t, t), t)
        kb = k_ref[rows, lanes]
        vb = v_ref[rows, lanes]
        z = lax.dot_general(qs, kb, (((1,), (1,)), ((), ())),
                            preferred_element_type=F32)
        if diagonal:
            z = jnp.where(strict, z, NEG_BIG)
            carry = jnp.zeros((2 * t, 1), F32)
        else:
            carry = carry_ref[slot]
        sp = jnp.where(z > SB_LINEAR_ABOVE, z, jnp.log2(1.0 + jnp.exp2(z)))
        hi = sp.astype(BF16)
        between = jnp.dot(hi, tri, preferred_element_type=F32) + carry
        a = jnp.exp2(z - sp - between)
        av = jnp.dot(a.astype(BF16), vb, preferred_element_type=F32)
        acc_ref[slot] = av if diagonal else acc_ref[slot] + av
        carry = carry + jnp.sum(sp, axis=1, keepdims=True)
        carry_ref[slot] = carry
        return carry

    def sweep(back, blocks, diagonal):
        low = None
        for b in blocks:
            for p in range(pairs):
                carry = block(i * nb + b - back, b, p, diagonal)
                low = carry if low is None else jnp.minimum(low, carry)
        return jnp.min(low)

    everyone = range(nb)

    def cond(state):
        return jnp.logical_and(state[0] <= i * nb, state[1] < SB_ZERO_LOG2)

    def body(state):
        return state[0] + 1, sweep(state[0], everyone, False)

    back, _ = lax.while_loop(cond, body, (1, sweep(0, everyone, True)))

    for extra in range(1, nb):
        later = range(extra, nb)
        pending = carry_ref[later[0] * pairs]
        for slot in range(later[0] * pairs + 1, nb * pairs):
            pending = jnp.minimum(pending, carry_ref[slot])

        @pl.when(jnp.logical_and(back == i * nb + extra, jnp.min(pending) < SB_ZERO_LOG2))
        def _():
            sweep(i * nb + extra, later, False)

        back = jnp.where(back == i * nb + extra, back + 1, back)

    for b in range(nb):
        for p in range(pairs):
            acc = acc_ref[b * pairs + p]
            o_ref[b * t:(b + 1) * t, p * LANES:(p + 1) * LANES] = jnp.where(
                first, acc[:t], acc[t:]).astype(o_ref.dtype)


def _sb_attn(q, k, v, t):
    S, cols = q.shape
    pairs = cols // LANES
    nb = min(SB_BLOCKS_PER_STEP, S // t)
    step = pl.BlockSpec((nb * t, cols), lambda i: (i, 0))
    return pl.pallas_call(
        functools.partial(_sb_attn_kernel, t=t),
        grid=(S // (nb * t),),
        in_specs=[step, step, step],
        out_specs=step,
        out_shape=jax.ShapeDtypeStruct((S, cols), BF16),
        scratch_shapes=[pltpu.VMEM((S, cols), BF16), pltpu.VMEM((S, cols), BF16),
                        pltpu.VMEM((nb * pairs, 2 * t, LANES), F32),
                        pltpu.VMEM((nb * pairs, 2 * t, 1), F32)],
        compiler_params=pltpu.CompilerParams(
            dimension_semantics=("arbitrary",), vmem_limit_bytes=VMEM_LIMIT),
        name="sb_attn",
    )(q, k, v)


def _mem_kv_kernel(mem_ref, g_ref, w_ref, kv_ref):
    mn = _rms(mem_ref[...], g_ref[...]).astype(BF16)
    kv_ref[...] = jnp.dot(mn, w_ref[...], preferred_element_type=F32).astype(kv_ref.dtype)


def _mem_kv(mem, g, w):
    M = mem.shape[0]
    return pl.pallas_call(
        _mem_kv_kernel,
        out_shape=jax.ShapeDtypeStruct((M, w.shape[1]), BF16),
        compiler_params=pltpu.CompilerParams(vmem_limit_bytes=VMEM_LIMIT),
        name="mem_kv",
    )(mem, g, w)


def _post_mlp_kernel(x_ref, ya_ref, yb_ref, wout_ref, gc_ref, wxq_ref, kv_ref, wxo_ref,
                     gm_ref, wup_ref, wdown_ref, gf_ref, o_ref, *, final_norm):
    D = x_ref.shape[1]
    half = ya_ref.shape[1]
    h = (x_ref[...]
         + jnp.dot(ya_ref[...], wout_ref[:half], preferred_element_type=F32)
         + jnp.dot(yb_ref[...], wout_ref[half:], preferred_element_type=F32))

    hn = _rms(h, gc_ref[...]).astype(BF16)
    xd = D // X_HEADS
    q = jnp.dot(hn, wxq_ref[...], preferred_element_type=F32) * (1.0 / math.sqrt(xd))
    q = q.astype(BF16)
    upd = jnp.zeros_like(h)
    for hd in range(X_HEADS):
        qh = q[:, hd * xd:(hd + 1) * xd]
        kh = kv_ref[:, hd * xd:(hd + 1) * xd]
        vh = kv_ref[:, D + hd * xd:D + (hd + 1) * xd]
        s = lax.dot_general(qh, kh, (((1,), (1,)), ((), ())), preferred_element_type=F32)
        p = jnp.exp(s - jnp.max(s, axis=-1, keepdims=True))
        l = jnp.sum(p, axis=-1, keepdims=True)
        oh = jnp.dot(p.astype(BF16), vh, preferred_element_type=F32) / l
        upd = upd + jnp.dot(oh.astype(BF16), wxo_ref[hd * xd:(hd + 1) * xd, :],
                            preferred_element_type=F32)
    h = h + upd

    hn = _rms(h, gm_ref[...]).astype(BF16)
    upd = jnp.zeros_like(h)
    for c in range(wup_ref.shape[1] // D):
        a = jnp.dot(hn, wup_ref[:, c * D:(c + 1) * D], preferred_element_type=F32)
        a = jnp.square(jnp.maximum(a, 0.0)).astype(BF16)
        upd = upd + jnp.dot(a, wdown_ref[c * D:(c + 1) * D, :], preferred_element_type=F32)
    h = h + upd
    if final_norm:
        h = _rms(h, gf_ref[...])
    o_ref[...] = h


def _post_mlp(x, ya, yb, kv, p, gf, final_norm):
    S, D = x.shape
    ts = min(ROW_TILE, S)
    row = lambda i: (i, 0)
    consts = [p["w_out"], p["g_cross"], p["w_xq"], kv, p["w_xo"],
              p["g_mlp"], p["w_up"], p["w_down"], gf]
    return pl.pallas_call(
        functools.partial(_post_mlp_kernel, final_norm=final_norm),
        grid=(S // ts,),
        in_specs=[pl.BlockSpec((ts, D), row),
                  pl.BlockSpec((ts, ya.shape[1]), row),
                  pl.BlockSpec((ts, yb.shape[1]), row)] + [_const_spec(c.shape) for c in consts],
        out_specs=pl.BlockSpec((ts, D), row),
        out_shape=jax.ShapeDtypeStruct((S, D), F32),
        compiler_params=pltpu.CompilerParams(
            dimension_semantics=("arbitrary",), vmem_limit_bytes=VMEM_LIMIT),
        name="post_mlp",
    )(x, ya, yb, *consts)


def _rope_inv_freq():
    inv = np.float32(ROPE_THETA) ** (-np.arange(0, ROT_DIM, 2, dtype=np.float32) / np.float32(ROT_DIM))
    return jnp.asarray(inv.reshape(ROT_DIM // 2, 1))


def _layer(h, pos_row, invf, mem, p, lam_init, gf, final_norm):
    S, D = h.shape
    td = min(DIFF_TILE, S)
    assert S % td == 0 and S % min(SB_TILE * SB_BLOCKS_PER_STEP, S) == 0 and S % min(ROW_TILE, S) == 0, S
    qt, ka, vtblk, qs, ks, vs, kn = _in_proj(h, pos_row, invf, p["g_mix"], p["w_in"], td)
    ya = _diff_attn(p["lam_vecs"], p["g_subln"], kn, qt, ka, vtblk, lam_init)
    yb = _sb_attn(qs, ks, vs, min(SB_TILE, S))
    kv = _mem_kv(mem, p["g_mem"], p["w_xkv"])
    return _post_mlp(h, ya, yb, kv, p, gf, final_norm)


def kernel(x, mem, positions, g_mix, w_in, lambda_q1, lambda_k1, lambda_q2, lambda_k2, g_subln, w_out, g_cross, g_mem, w_xq, w_xkv, w_xo, g_mlp, w_up, w_down, g_final):
    B, S, D = x.shape
    depth = w_in.shape[0]
    invf = _rope_inv_freq()
    gf = g_final.reshape(1, D).astype(F32)
    outs = []
    for b in range(B):
        h = x[b]
        pos_row = positions[b].reshape(1, S)
        for l in range(depth):
            lam_init = 0.8 - 0.6 * math.exp(-0.3 * l)
            p = {
                "g_mix": g_mix[l].reshape(1, D),
                "w_in": w_in[l].astype(BF16),
                "lam_vecs": jnp.stack([lambda_q1[l], lambda_k1[l], lambda_q2[l], lambda_k2[l]]).astype(F32),
                "g_subln": g_subln[l].reshape(DA_V, 1).astype(F32),
                "w_out": w_out[l].astype(BF16),
                "g_cross": g_cross[l].reshape(1, D),
                "g_mem": g_mem[l].reshape(1, D),
                "w_xq": w_xq[l].astype(BF16),
                "w_xkv": w_xkv[l].astype(BF16),
                "w_xo": w_xo[l].astype(BF16),
                "g_mlp": g_mlp[l].reshape(1, D),
                "w_up": w_up[l].astype(BF16),
                "w_down": w_down[l].astype(BF16),
            }
            h = _layer(h, pos_row, invf, mem[b], p, lam_init, gf, l == depth - 1)
        outs.append(h)
    return jnp.stack(outs)
```

```python
import functools
import math

import numpy as np
import jax
import jax.numpy as jnp
from jax import lax
from jax.experimental import pallas as pl
from jax.experimental.pallas import tpu as pltpu

F32 = jnp.float32
BF16 = jnp.bfloat16

HEAD_DIM = 64
DA_HEADS = 4
DA_V = 2 * HEAD_DIM
SB_HEADS = 8
GROUP_COLS = 512
ROT_DIM = HEAD_DIM // 4
ROPE_THETA = 500000.0
X_HEADS = 4
EPS = 1e-6
LANES = 128
NEG_BIG = -1e30

SB_ZERO_LOG2 = 151.0
SB_LINEAR_ABOVE = 64.0

LAGGED_MAX_SCORE = 45.0
KEY_NORM_SLACK = 1.02

VMEM_LIMIT = 56 * 1024 * 1024

ROW_TILE = 512
DIFF_TILE = 512
DIFF_HEADS_PER_STEP = 4
SB_TILE = 256
SB_BLOCKS_PER_STEP = 4
V_ROWS = DA_V + 16


def _rms(x, g):
    ms = jnp.mean(x * x, axis=-1, keepdims=True)
    return x * lax.rsqrt(ms + EPS) * g


def _const_spec(shape):
    nd = len(shape)
    return pl.BlockSpec(shape, lambda *_: (0,) * nd, pipeline_mode=pl.Buffered(1))


def _in_proj_kernel(x_ref, pos_ref, invf_ref, g_ref, w_ref,
                    qt_ref, ka_ref, vt_ref, qs_ref, ks_ref, vs_ref, kn_ref):
    ts = x_ref.shape[0]
    xn = _rms(x_ref[...], g_ref[...]).astype(BF16)

    half = ROT_DIM // 2
    ang = invf_ref[...] * pos_ref[...].astype(F32)
    cos8 = jnp.cos(ang)
    sin8 = jnp.sin(ang)
    one8 = jnp.ones_like(cos8)
    zero8 = jnp.zeros_like(cos8)
    groups = range(LANES // half)
    in_x1 = [g % (HEAD_DIM // half) == 0 for g in groups]
    in_x2 = [g % (HEAD_DIM // half) == 1 for g in groups]
    cos = jnp.concatenate([cos8 if a or b else one8 for a, b in zip(in_x1, in_x2)], axis=0).T
    s_up = jnp.concatenate([-sin8 if a else zero8 for a in in_x1], axis=0).T
    s_dn = jnp.concatenate([sin8 if b else zero8 for b in in_x2], axis=0).T

    def group(idx):
        w = w_ref[:, idx * GROUP_COLS:(idx + 1) * GROUP_COLS]
        return jnp.dot(xn, w, preferred_element_type=F32)

    def rope(u, scale):
        parts = []
        for c in range(GROUP_COLS // LANES):
            uc = u[:, c * LANES:(c + 1) * LANES]
            r = (uc * cos + pltpu.roll(uc, LANES - ROT_DIM // 2, axis=1) * s_up
                 + pltpu.roll(uc, ROT_DIM // 2, axis=1) * s_dn)
            parts.append(r * scale)
        return jnp.concatenate(parts, axis=1)

    q_scale = math.log2(math.e) / math.sqrt(HEAD_DIM)

    qt = rope(group(0), q_scale).T
    zero_half = jnp.zeros((HEAD_DIM, ts), BF16)
    for hc in range(2 * DA_HEADS):
        c = hc % 2
        piece = qt[hc * HEAD_DIM:(hc + 1) * HEAD_DIM].astype(BF16)
        qt_ref[hc * LANES + c * HEAD_DIM:hc * LANES + (c + 1) * HEAD_DIM, :] = piece
        qt_ref[hc * LANES + (1 - c) * HEAD_DIM:hc * LANES + (2 - c) * HEAD_DIM, :] = zero_half

    ka = rope(group(1), 1.0)
    ka_ref[...] = ka.astype(BF16)
    for h in range(DA_HEADS):
        kh = ka[:, h * LANES:(h + 1) * LANES]
        norm2 = jnp.max(jnp.sum(kh * kh, axis=1, keepdims=True), axis=0, keepdims=True)
        kn_ref[0, :, h * LANES:(h + 1) * LANES] = jnp.broadcast_to(norm2, (1, LANES))

    vt = group(2).astype(BF16).T
    pad_row = lax.broadcasted_iota(jnp.int32, (V_ROWS - DA_V, ts), 0)
    ones_pad = jnp.where(pad_row == 0, 1.0, 0.0).astype(BF16)
    for h in range(DA_HEADS):
        vt_ref[0, h * V_ROWS:h * V_ROWS + DA_V, :] = vt[h * DA_V:(h + 1) * DA_V]
        vt_ref[0, h * V_ROWS + DA_V:(h + 1) * V_ROWS, :] = ones_pad

    qs_ref[...] = (group(3) * q_scale).astype(BF16)
    ks_ref[...] = group(4).astype(BF16)
    vs_ref[...] = group(5).astype(BF16)


def _in_proj(x, pos_row, invf, g, w, ts):
    S, D = x.shape
    nt = S // ts
    flat = jax.ShapeDtypeStruct((S, GROUP_COLS), BF16)
    row = lambda i: (i, 0)
    flat_spec = pl.BlockSpec((ts, GROUP_COLS), row)
    return pl.pallas_call(
        _in_proj_kernel,
        grid=(nt,),
        in_specs=[
            pl.BlockSpec((ts, D), row),
            pl.BlockSpec((1, ts), lambda i: (0, i)),
            _const_spec(invf.shape),
            _const_spec((1, D)),
            _const_spec(w.shape),
        ],
        out_specs=[
            pl.BlockSpec((2 * DA_HEADS * LANES, ts), lambda i: (0, i)),
            flat_spec,
            pl.BlockSpec((1, DA_HEADS * V_ROWS, ts), lambda i: (i, 0, 0)),
            flat_spec, flat_spec, flat_spec,
            pl.BlockSpec((1, 1, DA_HEADS * LANES), lambda i: (i, 0, 0)),
        ],
        out_shape=[
            jax.ShapeDtypeStruct((2 * DA_HEADS * LANES, S), BF16),
            flat,
            jax.ShapeDtypeStruct((nt, DA_HEADS * V_ROWS, ts), BF16),
            flat, flat, flat,
            jax.ShapeDtypeStruct((nt, 1, DA_HEADS * LANES), F32),
        ],
        compiler_params=pltpu.CompilerParams(
            dimension_semantics=("arbitrary",), vmem_limit_bytes=VMEM_LIMIT),
        name="in_proj",
    )(x, pos_row, invf, g, w)


def _diff_attn_kernel(lam_ref, g_ref, kn_ref, bias_ref, qt_ref, k_ref, vt_ref, o_ref,
                      acc_ref, m_ref, pa_ref, pb_ref, ma_ref, mb_ref, fa_ref, fb_ref, *, lam_init):
    i = pl.program_id(1)
    t = qt_ref.shape[1]
    heads = qt_ref.shape[0] // (2 * LANES)
    maps = [(h, 2 * h + c) for h in range(heads) for c in range(2)]
    acc_ref[...] = jnp.zeros_like(acc_ref)

    def scores(j, h, mp, diagonal):
        kb = k_ref[pl.ds(pl.multiple_of(j * t, t), t), h * LANES:(h + 1) * LANES]
        st = jnp.dot(kb, qt_ref[mp * LANES:(mp + 1) * LANES, :], preferred_element_type=F32)
        if diagonal:
            st = st + bias_ref[...]
        return st

    def values(j, h):
        return vt_ref[j, h * V_ROWS:(h + 1) * V_ROWS, :]

    def exact_block(j, diagonal):
        for h, mp in maps:
            st = scores(j, h, mp, diagonal)
            m_old = m_ref[mp]
            m_new = jnp.maximum(m_old, jnp.max(st, axis=0, keepdims=True))
            alpha = jnp.exp2(m_old - m_new)
            p = jnp.exp2(st - m_new).astype(BF16)
            acc_ref[mp] = alpha * acc_ref[mp] + jnp.dot(values(j, h), p,
                                                        preferred_element_type=F32)
            m_ref[mp] = m_new

    def lagged_weights(j, p_ref, m_in_ref, m_out_ref, f_ref, diagonal=False):
        for h, mp in maps:
            st = scores(j, h, mp, diagonal)
            m_old = m_in_ref[mp]
            p_ref[mp] = jnp.exp2(st - m_old).astype(BF16)
            m_new = jnp.maximum(m_old, jnp.max(st, axis=0, keepdims=True))
            f_ref[mp] = jnp.exp2(m_old - m_new)
            m_out_ref[mp] = m_new

    def lagged_accumulate(j, p_ref, f_ref):
        for h, mp in maps:
            acc_ref[mp] = f_ref[mp] * (acc_ref[mp] + jnp.dot(values(j, h), p_ref[mp],
                                                            preferred_element_type=F32))

    tile = lax.broadcasted_iota(jnp.int32, kn_ref.shape, 0)
    kmax2 = jnp.max(jnp.where(tile <= i, kn_ref[...], 0.0)) * KEY_NORM_SLACK
    qn2 = jnp.zeros((1, t), F32)
    for _, mp in maps:
        qm = qt_ref[mp * LANES:(mp + 1) * LANES, :].astype(F32)
        qn2 = jnp.maximum(qn2, jnp.sum(qm * qm, axis=0, keepdims=True))
    safe = jnp.max(qn2) * kmax2 <= LAGGED_MAX_SCORE * LAGGED_MAX_SCORE

    @pl.when(safe)
    def _():
        mb_ref[...] = jnp.zeros_like(mb_ref)

        @pl.when(i == 0)
        def _():
            lagged_weights(0, pa_ref, mb_ref, ma_ref, fa_ref, diagonal=True)
            lagged_accumulate(0, pa_ref, fa_ref)

        @pl.when(i > 0)
        def _():
            lagged_weights(0, pa_ref, mb_ref, ma_ref, fa_ref)

            def pair(n, carry):
                j = 2 * n
                lagged_weights(j + 1, pb_ref, ma_ref, mb_ref, fb_ref)
                lagged_accumulate(j, pa_ref, fa_ref)
                lagged_weights(j + 2, pa_ref, mb_ref, ma_ref, fa_ref)
                lagged_accumulate(j + 1, pb_ref, fb_ref)
                return carry

            lax.fori_loop(0, (i - 1) // 2, pair, 0)

            @pl.when(i % 2 == 1)
            def _():
                lagged_weights(i, pb_ref, ma_ref, mb_ref, fb_ref, diagonal=True)
                lagged_accumulate(i - 1, pa_ref, fa_ref)
                lagged_accumulate(i, pb_ref, fb_ref)

            @pl.when(i % 2 == 0)
            def _():
                lagged_weights(i - 1, pb_ref, ma_ref, mb_ref, fb_ref)
                lagged_accumulate(i - 2, pa_ref, fa_ref)
                lagged_weights(i, pa_ref, mb_ref, ma_ref, fa_ref, diagonal=True)
                lagged_accumulate(i - 1, pb_ref, fb_ref)
                lagged_accumulate(i, pa_ref, fa_ref)

    @pl.when(jnp.logical_not(safe))
    def _():
        m_ref[...] = jnp.full_like(m_ref, NEG_BIG)

        def one(j, carry):
            exact_block(j, False)
            return carry

        lax.fori_loop(0, i, one, 0)
        exact_block(i, True)

    lam_v = lam_ref[...]
    lam = (jnp.exp(jnp.sum(lam_v[0:1] * lam_v[1:2], axis=1, keepdims=True))
           - jnp.exp(jnp.sum(lam_v[2:3] * lam_v[3:4], axis=1, keepdims=True)) + lam_init)
    for h in range(heads):
        a0 = acc_ref[2 * h]
        a1 = acc_ref[2 * h + 1]
        o = a0[:DA_V] / a0[DA_V:DA_V + 1] - lam * (a1[:DA_V] / a1[DA_V:DA_V + 1])
        ms = jnp.mean(o * o, axis=0, keepdims=True)
        y = o * lax.rsqrt(ms + EPS) * g_ref[...] * (1.0 - lam_init)
        o_ref[:, h * DA_V:(h + 1) * DA_V] = y.T.astype(o_ref.dtype)


def _diff_attn(lam_vecs, g_col, kn, qt, ka, vtblk, lam_init):
    S = ka.shape[0]
    nk, _, t = vtblk.shape
    key, qry = np.arange(t)[:, None], np.arange(t)[None, :]
    causal_bias = jnp.asarray(np.where(key <= qry, 0.0, NEG_BIG), F32)
    hg = DIFF_HEADS_PER_STEP
    nm = 2 * hg
    col_block = lambda shape: pl.BlockSpec(shape, lambda g, i: (0,) * (len(shape) - 1) + (g,),
                                           pipeline_mode=pl.Buffered(1))
    return pl.pallas_call(
        functools.partial(_diff_attn_kernel, lam_init=lam_init),
        grid=(DA_HEADS // hg, nk),
        in_specs=[
            _const_spec(lam_vecs.shape),
            _const_spec(g_col.shape),
            col_block((nk, 1, hg * LANES)),
            _const_spec((t, t)),
            pl.BlockSpec((nm * LANES, t), lambda g, i: (g, i)),
            col_block((S, hg * LANES)),
            pl.BlockSpec((nk, hg * V_ROWS, t), lambda g, i: (0, g, 0),
                         pipeline_mode=pl.Buffered(1)),
        ],
        out_specs=pl.BlockSpec((t, hg * DA_V), lambda g, i: (i, g)),
        out_shape=jax.ShapeDtypeStruct((S, DA_HEADS * DA_V), BF16),
        scratch_shapes=[pltpu.VMEM((nm, V_ROWS, t), F32), pltpu.VMEM((nm, 1, t), F32),
                        pltpu.VMEM((nm, t, t), BF16), pltpu.VMEM((nm, t, t), BF16),
                        pltpu.VMEM((nm, 1, t), F32), pltpu.VMEM((nm, 1, t), F32),
                        pltpu.VMEM((nm, 1, t), F32), pltpu.VMEM((nm, 1, t), F32)],
        compiler_params=pltpu.CompilerParams(
            dimension_semantics=("arbitrary", "arbitrary"), vmem_limit_bytes=VMEM_LIMIT),
        name="diff_attn",
    )(lam_vecs, g_col, kn, causal_bias, qt, ka, vtblk)


def _sb_attn_kernel(q_ref, knew_ref, vnew_ref, o_ref, k_ref, v_ref, acc_ref, carry_ref, *, t):
    i = pl.program_id(0)
    nb = q_ref.shape[0] // t
    pairs = q_ref.shape[1] // LANES
    own = pl.ds(pl.multiple_of(i * nb * t, nb * t), nb * t)
    k_ref[own, :] = knew_ref[...]
    v_ref[own, :] = vnew_ref[...]
    lane = lax.broadcasted_iota(jnp.int32, (t, LANES), 1)
    first = lane < HEAD_DIM
    row = lax.broadcasted_iota(jnp.int32, (t, t), 0)
    col = lax.broadcasted_iota(jnp.int32, (t, t), 1)
    tri = jnp.where(row > col, 1.0, 0.0).astype(BF16)
    strict = jnp.concatenate([col < row, col < row], axis=0)

    def block(j, b, p, diagonal):
        lanes = slice(p * LANES, (p + 1) * LANES)
        slot = b * pairs + p
        q2 = q_ref[b * t:(b + 1) * t, lanes]
        zeros = jnp.zeros_like(q2)
        qs = jnp.concatenate([jnp.where(first, q2, zeros), jnp.where(first, zeros, q2)], axis=0)
        rows = pl.ds(pl.multiple_of(j * t, t), t)
        kb = k_ref[rows, lanes]
        vb = v_ref[rows, lanes]
        z = lax.dot_general(qs, kb, (((1,), (1,)), ((), ())),
                            preferred_element_type=F32)
        if diagonal:
            z = jnp.where(strict, z, NEG_BIG)
            carry = jnp.zeros((2 * t, 1), F32)
        else:
            carry = carry_ref[slot]
        sp = jnp.where(z > SB_LINEAR_ABOVE, z, jnp.log2(1.0 + jnp.exp2(z)))
        hi = sp.astype(BF16)
        between = jnp.dot(hi, tri, preferred_element_type=F32) + carry
        a = jnp.exp2(z - sp - between)
        av = jnp.dot(a.astype(BF16), vb, preferred_element_type=F32)
        acc_ref[slot] = av if diagonal else acc_ref[slot] + av
        carry = carry + jnp.sum(sp, axis=1, keepdims=True)
        carry_ref[slot] = carry
        return carry

    def sweep(back, blocks, diagonal):
        low = None
        for b in blocks:
            for p in range(pairs):
                carry = block(i * nb + b - back, b, p, diagonal)
                low = carry if low is None else jnp.minimum(low, carry)
        return jnp.min(low)

    everyone = range(nb)

    def cond(state):
        return jnp.logical_and(state[0] <= i * nb, state[1] < SB_ZERO_LOG2)

    def body(state):
        return state[0] + 1, sweep(state[0], everyone, False)

    back, _ = lax.while_loop(cond, body, (1, sweep(0, everyone, True)))

    for extra in range(1, nb):
        later = range(extra, nb)
        pending = carry_ref[later[0] * pairs]
        for slot in range(later[0] * pairs + 1, nb * pairs):
            pending = jnp.minimum(pending, carry_ref[slot])

        @pl.when(jnp.logical_and(back == i * nb + extra, jnp.min(pending) < SB_ZERO_LOG2))
        def _():
            sweep(i * nb + extra, later, False)

        back = jnp.where(back == i * nb + extra, back + 1, back)

    for b in range(nb):
        for p in range(pairs):
            acc = acc_ref[b * pairs + p]
            o_ref[b * t:(b + 1) * t, p * LANES:(p + 1) * LANES] = jnp.where(
                first, acc[:t], acc[t:]).astype(o_ref.dtype)


def _sb_attn(q, k, v, t):
    S, cols = q.shape
    pairs = cols // LANES
    nb = min(SB_BLOCKS_PER_STEP, S // t)
    step = pl.BlockSpec((nb * t, cols), lambda i: (i, 0))
    return pl.pallas_call(
        functools.partial(_sb_attn_kernel, t=t),
        grid=(S // (nb * t),),
        in_specs=[step, step, step],
        out_specs=step,
        out_shape=jax.ShapeDtypeStruct((S, cols), BF16),
        scratch_shapes=[pltpu.VMEM((S, cols), BF16), pltpu.VMEM((S, cols), BF16),
                        pltpu.VMEM((nb * pairs, 2 * t, LANES), F32),
                        pltpu.VMEM((nb * pairs, 2 * t, 1), F32)],
        compiler_params=pltpu.CompilerParams(
            dimension_semantics=("arbitrary",), vmem_limit_bytes=VMEM_LIMIT),
        name="sb_attn",
    )(q, k, v)


def _mem_kv_kernel(mem_ref, g_ref, w_ref, kv_ref):
    mn = _rms(mem_ref[...], g_ref[...]).astype(BF16)
    kv_ref[...] = jnp.dot(mn, w_ref[...], preferred_element_type=F32).astype(kv_ref.dtype)


def _mem_kv(mem, g, w):
    M = mem.shape[0]
    return pl.pallas_call(
        _mem_kv_kernel,
        out_shape=jax.ShapeDtypeStruct((M, w.shape[1]), BF16),
        compiler_params=pltpu.CompilerParams(vmem_limit_bytes=VMEM_LIMIT),
        name="mem_kv",
    )(mem, g, w)


def _post_mlp_kernel(x_ref, ya_ref, yb_ref, wout_ref, gc_ref, wxq_ref, kv_ref, wxo_ref,
                     gm_ref, wup_ref, wdown_ref, gf_ref, o_ref, *, final_norm):
    D = x_ref.shape[1]
    y = jnp.concatenate([ya_ref[...], yb_ref[...]], axis=1)
    h = x_ref[...] + jnp.dot(y, wout_ref[...], preferred_element_type=F32)

    hn = _rms(h, gc_ref[...]).astype(BF16)
    xd = D // X_HEADS
    q = jnp.dot(hn, wxq_ref[...], preferred_element_type=F32) * (math.log2(math.e) / math.sqrt(xd))
    q = q.astype(BF16)
    outs = []
    for hd in range(X_HEADS):
        qh = q[:, hd * xd:(hd + 1) * xd]
        kh = kv_ref[:, hd * xd:(hd + 1) * xd]
        vh = kv_ref[:, D + hd * xd:D + (hd + 1) * xd]
        s = lax.dot_general(qh, kh, (((1,), (1,)), ((), ())), preferred_element_type=F32)
        p = jnp.exp2(s - jnp.max(s, axis=-1, keepdims=True))
        l = jnp.sum(p, axis=-1, keepdims=True)
        outs.append((jnp.dot(p.astype(BF16), vh, preferred_element_type=F32) / l).astype(BF16))
    h = h + jnp.dot(jnp.concatenate(outs, axis=1), wxo_ref[...], preferred_element_type=F32)

    hn = _rms(h, gm_ref[...]).astype(BF16)
    acts = []
    for c in range(wup_ref.shape[1] // D):
        a = jnp.dot(hn, wup_ref[:, c * D:(c + 1) * D], preferred_element_type=F32)
        acts.append(jnp.square(jnp.maximum(a, 0.0)).astype(BF16))
    h = h + jnp.dot(jnp.concatenate(acts, axis=1), wdown_ref[...], preferred_element_type=F32)
    if final_norm:
        h = _rms(h, gf_ref[...])
    o_ref[...] = h


def _post_mlp(x, ya, yb, kv, p, gf, final_norm):
    S, D = x.shape
    ts = min(ROW_TILE, S)
    row = lambda i: (i, 0)
    consts = [p["w_out"], p["g_cross"], p["w_xq"], kv, p["w_xo"],
              p["g_mlp"], p["w_up"], p["w_down"], gf]
    return pl.pallas_call(
        functools.partial(_post_mlp_kernel, final_norm=final_norm),
        grid=(S // ts,),
        in_specs=[pl.BlockSpec((ts, D), row),
                  pl.BlockSpec((ts, ya.shape[1]), row),
                  pl.BlockSpec((ts, yb.shape[1]), row)] + [_const_spec(c.shape) for c in consts],
        out_specs=pl.BlockSpec((ts, D), row),
        out_shape=jax.ShapeDtypeStruct((S, D), F32),
        compiler_params=pltpu.CompilerParams(
            dimension_semantics=("arbitrary",), vmem_limit_bytes=VMEM_LIMIT),
        name="post_mlp",
    )(x, ya, yb, *consts)


def _rope_inv_freq():
    inv = np.float32(ROPE_THETA) ** (-np.arange(0, ROT_DIM, 2, dtype=np.float32) / np.float32(ROT_DIM))
    return jnp.asarray(inv.reshape(ROT_DIM // 2, 1))


def _layer(h, pos_row, invf, mem, p, lam_init, gf, final_norm):
    S, D = h.shape
    td = min(DIFF_TILE, S)
    assert S % td == 0 and S % min(SB_TILE * SB_BLOCKS_PER_STEP, S) == 0 and S % min(ROW_TILE, S) == 0, S
    qt, ka, vtblk, qs, ks, vs, kn = _in_proj(h, pos_row, invf, p["g_mix"], p["w_in"], td)
    ya = _diff_attn(p["lam_vecs"], p["g_subln"], kn, qt, ka, vtblk, lam_init)
    yb = _sb_attn(qs, ks, vs, min(SB_TILE, S))
    kv = _mem_kv(mem, p["g_mem"], p["w_xkv"])
    return _post_mlp(h, ya, yb, kv, p, gf, final_norm)


def kernel(x, mem, positions, g_mix, w_in, lambda_q1, lambda_k1, lambda_q2, lambda_k2, g_subln, w_out, g_cross, g_mem, w_xq, w_xkv, w_xo, g_mlp, w_up, w_down, g_final):
    B, S, D = x.shape
    depth = w_in.shape[0]
    invf = _rope_inv_freq()
    gf = g_final.reshape(1, D).astype(F32)
    outs = []
    for b in range(B):
        h = x[b]
        pos_row = positions[b].reshape(1, S)
        for l in range(depth):
            lam_init = 0.8 - 0.6 * math.exp(-0.3 * l)
            p = {
                "g_mix": g_mix[l].reshape(1, D),
                "w_in": w_in[l].astype(BF16),
                "lam_vecs": jnp.stack([lambda_q1[l], lambda_k1[l], lambda_q2[l], lambda_k2[l]]).astype(F32),
                "g_subln": g_subln[l].reshape(DA_V, 1).astype(F32),
                "w_out": w_out[l].astype(BF16),
                "g_cross": g_cross[l].reshape(1, D),
                "g_mem": g_mem[l].reshape(1, D),
                "w_xq": w_xq[l].astype(BF16),
                "w_xkv": w_xkv[l].astype(BF16),
                "w_xo": w_xo[l].astype(BF16),
                "g_mlp": g_mlp[l].reshape(1, D),
                "w_up": w_up[l].astype(BF16),
                "w_down": w_down[l].astype(BF16),
            }
            h = _layer(h, pos_row, invf, mem[b], p, lam_init, gf, l == depth - 1)
        outs.append(h)
    return jnp.stack(outs)
```

```python
import functools
import math

import numpy as np
import jax
import jax.numpy as jnp
from jax import lax
from jax.experimental import pallas as pl
from jax.experimental.pallas import tpu as pltpu

F32 = jnp.float32
BF16 = jnp.bfloat16

HEAD_DIM = 64
DA_HEADS = 4
DA_V = 2 * HEAD_DIM
SB_HEADS = 8
GROUP_COLS = 512
ROT_DIM = HEAD_DIM // 4
ROPE_THETA = 500000.0
X_HEADS = 4
EPS = 1e-6
LANES = 128
NEG_BIG = -1e30

SB_ZERO_LOG2 = 151.0
SB_LINEAR_ABOVE = 64.0

LAGGED_MAX_SCORE = 45.0
KEY_NORM_SLACK = 1.02

VMEM_LIMIT = 56 * 1024 * 1024

ROW_TILE = 512
DIFF_TILE = 512
DIFF_HEADS_PER_STEP = 4
SB_TILE = 256
SB_BLOCKS_PER_STEP = 2
V_ROWS = DA_V + 16


def _rms(x, g):
    ms = jnp.mean(x * x, axis=-1, keepdims=True)
    return x * lax.rsqrt(ms + EPS) * g


def _const_spec(shape):
    nd = len(shape)
    return pl.BlockSpec(shape, lambda *_: (0,) * nd, pipeline_mode=pl.Buffered(1))


def _in_proj_kernel(x_ref, pos_ref, invf_ref, g_ref, w_ref,
                    qt_ref, ka_ref, vt_ref, qs_ref, ks_ref, vs_ref, kn_ref, wb_ref):
    ts = x_ref.shape[0]

    @pl.when(pl.program_id(0) == 0)
    def _():
        for idx in range(w_ref.shape[1] // GROUP_COLS):
            cols = slice(idx * GROUP_COLS, (idx + 1) * GROUP_COLS)
            wb_ref[:, cols] = w_ref[:, cols].astype(BF16)

    xn = _rms(x_ref[...], g_ref[...]).astype(BF16)

    half = ROT_DIM // 2
    ang = invf_ref[...] * pos_ref[...].astype(F32)
    cos8 = jnp.cos(ang)
    sin8 = jnp.sin(ang)
    one8 = jnp.ones_like(cos8)
    zero8 = jnp.zeros_like(cos8)
    groups = range(LANES // half)
    in_x1 = [g % (HEAD_DIM // half) == 0 for g in groups]
    in_x2 = [g % (HEAD_DIM // half) == 1 for g in groups]
    cos = jnp.concatenate([cos8 if a or b else one8 for a, b in zip(in_x1, in_x2)], axis=0).T
    s_up = jnp.concatenate([-sin8 if a else zero8 for a in in_x1], axis=0).T
    s_dn = jnp.concatenate([sin8 if b else zero8 for b in in_x2], axis=0).T

    def group(idx):
        w = wb_ref[:, idx * GROUP_COLS:(idx + 1) * GROUP_COLS]
        return jnp.dot(xn, w, preferred_element_type=F32)

    def rope(u, scale):
        parts = []
        for c in range(GROUP_COLS // LANES):
            uc = u[:, c * LANES:(c + 1) * LANES]
            r = (uc * cos + pltpu.roll(uc, LANES - ROT_DIM // 2, axis=1) * s_up
                 + pltpu.roll(uc, ROT_DIM // 2, axis=1) * s_dn)
            parts.append(r * scale)
        return jnp.concatenate(parts, axis=1)

    q_scale = math.log2(math.e) / math.sqrt(HEAD_DIM)

    qt = rope(group(0), q_scale).T
    zero_half = jnp.zeros((HEAD_DIM, ts), BF16)
    for hc in range(2 * DA_HEADS):
        c = hc % 2
        piece = qt[hc * HEAD_DIM:(hc + 1) * HEAD_DIM].astype(BF16)
        qt_ref[hc * LANES + c * HEAD_DIM:hc * LANES + (c + 1) * HEAD_DIM, :] = piece
        qt_ref[hc * LANES + (1 - c) * HEAD_DIM:hc * LANES + (2 - c) * HEAD_DIM, :] = zero_half

    ka = rope(group(1), 1.0)
    ka_ref[...] = ka.astype(BF16)
    for h in range(DA_HEADS):
        kh = ka[:, h * LANES:(h + 1) * LANES]
        norm2 = jnp.max(jnp.sum(kh * kh, axis=1, keepdims=True), axis=0, keepdims=True)
        kn_ref[0, :, h * LANES:(h + 1) * LANES] = jnp.broadcast_to(norm2, (1, LANES))

    vt = group(2).astype(BF16).T
    pad_row = lax.broadcasted_iota(jnp.int32, (V_ROWS - DA_V, ts), 0)
    ones_pad = jnp.where(pad_row == 0, 1.0, 0.0).astype(BF16)
    for h in range(DA_HEADS):
        vt_ref[0, h * V_ROWS:h * V_ROWS + DA_V, :] = vt[h * DA_V:(h + 1) * DA_V]
        vt_ref[0, h * V_ROWS + DA_V:(h + 1) * V_ROWS, :] = ones_pad

    qs_ref[...] = (group(3) * q_scale).astype(BF16)
    ks_ref[...] = group(4).astype(BF16)
    vs_ref[...] = group(5).astype(BF16)


def _in_proj(x, pos_row, invf, g, w, ts):
    S, D = x.shape
    nt = S // ts
    flat = jax.ShapeDtypeStruct((S, GROUP_COLS), BF16)
    row = lambda i: (i, 0)
    flat_spec = pl.BlockSpec((ts, GROUP_COLS), row)
    return pl.pallas_call(
        _in_proj_kernel,
        grid=(nt,),
        in_specs=[
            pl.BlockSpec((ts, D), row),
            pl.BlockSpec((1, ts), lambda i: (0, i)),
            _const_spec(invf.shape),
            _const_spec((1, D)),
            _const_spec(w.shape),
        ],
        out_specs=[
            pl.BlockSpec((2 * DA_HEADS * LANES, ts), lambda i: (0, i)),
            flat_spec,
            pl.BlockSpec((1, DA_HEADS * V_ROWS, ts), lambda i: (i, 0, 0)),
            flat_spec, flat_spec, flat_spec,
            pl.BlockSpec((1, 1, DA_HEADS * LANES), lambda i: (i, 0, 0)),
        ],
        out_shape=[
            jax.ShapeDtypeStruct((2 * DA_HEADS * LANES, S), BF16),
            flat,
            jax.ShapeDtypeStruct((nt, DA_HEADS * V_ROWS, ts), BF16),
            flat, flat, flat,
            jax.ShapeDtypeStruct((nt, 1, DA_HEADS * LANES), F32),
        ],
        scratch_shapes=[pltpu.VMEM(w.shape, BF16)],
        compiler_params=pltpu.CompilerParams(
            dimension_semantics=("arbitrary",), vmem_limit_bytes=VMEM_LIMIT),
        name="in_proj",
    )(x, pos_row, invf, g, w)


def _diff_attn_kernel(lam_ref, g_ref, kn_ref, bias_ref, qt_ref, k_ref, vt_ref, o_ref,
                      acc_ref, m_ref, pa_ref, pb_ref, ma_ref, mb_ref, fa_ref, fb_ref, *, lam_init):
    i = pl.program_id(1)
    t = qt_ref.shape[1]
    heads = qt_ref.shape[0] // (2 * LANES)
    maps = [(h, 2 * h + c) for h in range(heads) for c in range(2)]
    acc_ref[...] = jnp.zeros_like(acc_ref)

    def scores(j, h, mp, diagonal):
        kb = k_ref[pl.ds(pl.multiple_of(j * t, t), t), h * LANES:(h + 1) * LANES]
        st = jnp.dot(kb, qt_ref[mp * LANES:(mp + 1) * LANES, :], preferred_element_type=F32)
        if diagonal:
            st = st + bias_ref[...]
        return st

    def values(j, h):
        return vt_ref[j, h * V_ROWS:(h + 1) * V_ROWS, :]

    def exact_block(j, diagonal):
        for h, mp in maps:
            st = scores(j, h, mp, diagonal)
            m_old = m_ref[mp]
            m_new = jnp.maximum(m_old, jnp.max(st, axis=0, keepdims=True))
            alpha = jnp.exp2(m_old - m_new)
            p = jnp.exp2(st - m_new).astype(BF16)
            acc_ref[mp] = alpha * acc_ref[mp] + jnp.dot(values(j, h), p,
                                                        preferred_element_type=F32)
            m_ref[mp] = m_new

    def lagged_weights(j, p_ref, m_in_ref, m_out_ref, f_ref, diagonal=False):
        for h, mp in maps:
            st = scores(j, h, mp, diagonal)
            m_old = m_in_ref[mp]
            p_ref[mp] = jnp.exp2(st - m_old).astype(BF16)
            m_new = jnp.maximum(m_old, jnp.max(st, axis=0, keepdims=True))
            f_ref[mp] = jnp.exp2(m_old - m_new)
            m_out_ref[mp] = m_new

    def lagged_accumulate(j, p_ref, f_ref):
        for h, mp in maps:
            acc_ref[mp] = f_ref[mp] * (acc_ref[mp] + jnp.dot(values(j, h), p_ref[mp],
                                                            preferred_element_type=F32))

    tile = lax.broadcasted_iota(jnp.int32, kn_ref.shape, 0)
    kmax2 = jnp.max(jnp.where(tile <= i, kn_ref[...], 0.0)) * KEY_NORM_SLACK
    qn2 = jnp.zeros((1, t), F32)
    for _, mp in maps:
        qm = qt_ref[mp * LANES:(mp + 1) * LANES, :].astype(F32)
        qn2 = jnp.maximum(qn2, jnp.sum(qm * qm, axis=0, keepdims=True))
    safe = jnp.max(qn2) * kmax2 <= LAGGED_MAX_SCORE * LAGGED_MAX_SCORE

    @pl.when(safe)
    def _():
        mb_ref[...] = jnp.zeros_like(mb_ref)

        @pl.when(i == 0)
        def _():
            lagged_weights(0, pa_ref, mb_ref, ma_ref, fa_ref, diagonal=True)
            lagged_accumulate(0, pa_ref, fa_ref)

        @pl.when(i > 0)
        def _():
            lagged_weights(0, pa_ref, mb_ref, ma_ref, fa_ref)

            def pair(n, carry):
                j = 2 * n
                lagged_weights(j + 1, pb_ref, ma_ref, mb_ref, fb_ref)
                lagged_accumulate(j, pa_ref, fa_ref)
                lagged_weights(j + 2, pa_ref, mb_ref, ma_ref, fa_ref)
                lagged_accumulate(j + 1, pb_ref, fb_ref)
                return carry

            lax.fori_loop(0, (i - 1) // 2, pair, 0)

            @pl.when(i % 2 == 1)
            def _():
                lagged_weights(i, pb_ref, ma_ref, mb_ref, fb_ref, diagonal=True)
                lagged_accumulate(i - 1, pa_ref, fa_ref)
                lagged_accumulate(i, pb_ref, fb_ref)

            @pl.when(i % 2 == 0)
            def _():
                lagged_weights(i - 1, pb_ref, ma_ref, mb_ref, fb_ref)
                lagged_accumulate(i - 2, pa_ref, fa_ref)
                lagged_weights(i, pa_ref, mb_ref, ma_ref, fa_ref, diagonal=True)
                lagged_accumulate(i - 1, pb_ref, fb_ref)
                lagged_accumulate(i, pa_ref, fa_ref)

    @pl.when(jnp.logical_not(safe))
    def _():
        m_ref[...] = jnp.full_like(m_ref, NEG_BIG)

        def one(j, carry):
            exact_block(j, False)
            return carry

        lax.fori_loop(0, i, one, 0)
        exact_block(i, True)

    lam_v = lam_ref[...]
    lam = (jnp.exp(jnp.sum(lam_v[0:1] * lam_v[1:2], axis=1, keepdims=True))
           - jnp.exp(jnp.sum(lam_v[2:3] * lam_v[3:4], axis=1, keepdims=True)) + lam_init)
    for h in range(heads):
        a0 = acc_ref[2 * h]
        a1 = acc_ref[2 * h + 1]
        o = a0[:DA_V] / a0[DA_V:DA_V + 1] - lam * (a1[:DA_V] / a1[DA_V:DA_V + 1])
        ms = jnp.mean(o * o, axis=0, keepdims=True)
        y = o * lax.rsqrt(ms + EPS) * g_ref[...] * (1.0 - lam_init)
        o_ref[:, h * DA_V:(h + 1) * DA_V] = y.T.astype(o_ref.dtype)


def _diff_attn(lam_vecs, g_col, kn, qt, ka, vtblk, lam_init):
    S = ka.shape[0]
    nk, _, t = vtblk.shape
    key, qry = np.arange(t)[:, None], np.arange(t)[None, :]
    causal_bias = jnp.asarray(np.where(key <= qry, 0.0, NEG_BIG), F32)
    hg = DIFF_HEADS_PER_STEP
    nm = 2 * hg
    col_block = lambda shape: pl.BlockSpec(shape, lambda g, i: (0,) * (len(shape) - 1) + (g,),
                                           pipeline_mode=pl.Buffered(1))
    return pl.pallas_call(
        functools.partial(_diff_attn_kernel, lam_init=lam_init),
        grid=(DA_HEADS // hg, nk),
        in_specs=[
            _const_spec(lam_vecs.shape),
            _const_spec(g_col.shape),
            col_block((nk, 1, hg * LANES)),
            _const_spec((t, t)),
            pl.BlockSpec((nm * LANES, t), lambda g, i: (g, i)),
            col_block((S, hg * LANES)),
            pl.BlockSpec((nk, hg * V_ROWS, t), lambda g, i: (0, g, 0),
                         pipeline_mode=pl.Buffered(1)),
        ],
        out_specs=pl.BlockSpec((t, hg * DA_V), lambda g, i: (i, g)),
        out_shape=jax.ShapeDtypeStruct((S, DA_HEADS * DA_V), BF16),
        scratch_shapes=[pltpu.VMEM((nm, V_ROWS, t), F32), pltpu.VMEM((nm, 1, t), F32),
                        pltpu.VMEM((nm, t, t), BF16), pltpu.VMEM((nm, t, t), BF16),
                        pltpu.VMEM((nm, 1, t), F32), pltpu.VMEM((nm, 1, t), F32),
                        pltpu.VMEM((nm, 1, t), F32), pltpu.VMEM((nm, 1, t), F32)],
        compiler_params=pltpu.CompilerParams(
            dimension_semantics=("arbitrary", "arbitrary"), vmem_limit_bytes=VMEM_LIMIT),
        name="diff_attn",
    )(lam_vecs, g_col, kn, causal_bias, qt, ka, vtblk)


def _sb_attn_kernel(q_ref, knew_ref, vnew_ref, o_ref, k_ref, v_ref, acc_ref, carry_ref, *, t):
    i = pl.program_id(0)
    nb = q_ref.shape[0] // t
    pairs = q_ref.shape[1] // LANES
    own = pl.ds(pl.multiple_of(i * nb * t, nb * t), nb * t)
    k_ref[own, :] = knew_ref[...]
    v_ref[own, :] = vnew_ref[...]
    lane = lax.broadcasted_iota(jnp.int32, (t, LANES), 1)
    first = lane < HEAD_DIM
    row = lax.broadcasted_iota(jnp.int32, (t, t), 0)
    col = lax.broadcasted_iota(jnp.int32, (t, t), 1)
    tri = jnp.where(row > col, 1.0, 0.0).astype(BF16)
    strict = jnp.concatenate([col < row, col < row], axis=0)

    def block(j, b, p, diagonal):
        lanes = slice(p * LANES, (p + 1) * LANES)
        slot = b * pairs + p
        q2 = q_ref[b * t:(b + 1) * t, lanes]
        zeros = jnp.zeros_like(q2)
        qs = jnp.concatenate([jnp.where(first, q2, zeros), jnp.where(first, zeros, q2)], axis=0)
        rows = pl.ds(pl.multiple_of(j * t, t), t)
        kb = k_ref[rows, lanes]
        vb = v_ref[rows, lanes]
        z = lax.dot_general(qs, kb, (((1,), (1,)), ((), ())),
                            preferred_element_type=F32)
        if diagonal:
            z = jnp.where(strict, z, NEG_BIG)
            carry = jnp.zeros((2 * t, 1), F32)
        else:
            carry = carry_ref[slot]
        sp = jnp.where(z > SB_LINEAR_ABOVE, z, jnp.log2(1.0 + jnp.exp2(z)))
        hi = sp.astype(BF16)
        between = jnp.dot(hi, tri, preferred_element_type=F32) + carry
        a = jnp.exp2(z - sp - between)
        av = jnp.dot(a.astype(BF16), vb, preferred_element_type=F32)
        acc_ref[slot] = av if diagonal else acc_ref[slot] + av
        carry = carry + jnp.sum(sp, axis=1, keepdims=True)
        carry_ref[slot] = carry
        return carry

    def sweep(back, blocks, diagonal):
        low = None
        for b in blocks:
            for p in range(pairs):
                carry = block(i * nb + b - back, b, p, diagonal)
                low = carry if low is None else jnp.minimum(low, carry)
        return jnp.min(low)

    everyone = range(nb)

    def cond(state):
        return jnp.logical_and(state[0] <= i * nb, state[1] < SB_ZERO_LOG2)

    def body(state):
        return state[0] + 1, sweep(state[0], everyone, False)

    back, _ = lax.while_loop(cond, body, (1, sweep(0, everyone, True)))

    for extra in range(1, nb):
        later = range(extra, nb)
        pending = carry_ref[later[0] * pairs]
        for slot in range(later[0] * pairs + 1, nb * pairs):
            pending = jnp.minimum(pending, carry_ref[slot])

        @pl.when(jnp.logical_and(back == i * nb + extra, jnp.min(pending) < SB_ZERO_LOG2))
        def _():
            sweep(i * nb + extra, later, False)

        back = jnp.where(back == i * nb + extra, back + 1, back)

    for b in range(nb):
        for p in range(pairs):
            acc = acc_ref[b * pairs + p]
            o_ref[b * t:(b + 1) * t, p * LANES:(p + 1) * LANES] = jnp.where(
                first, acc[:t], acc[t:]).astype(o_ref.dtype)


def _sb_attn(q, k, v, t):
    S, cols = q.shape
    pairs = cols // LANES
    nb = min(SB_BLOCKS_PER_STEP, S // t)
    step = pl.BlockSpec((nb * t, cols), lambda i: (i, 0))
    return pl.pallas_call(
        functools.partial(_sb_attn_kernel, t=t),
        grid=(S // (nb * t),),
        in_specs=[step, step, step],
        out_specs=step,
        out_shape=jax.ShapeDtypeStruct((S, cols), BF16),
        scratch_shapes=[pltpu.VMEM((S, cols), BF16), pltpu.VMEM((S, cols), BF16),
                        pltpu.VMEM((nb * pairs, 2 * t, LANES), F32),
                        pltpu.VMEM((nb * pairs, 2 * t, 1), F32)],
        compiler_params=pltpu.CompilerParams(
            dimension_semantics=("arbitrary",), vmem_limit_bytes=VMEM_LIMIT),
        name="sb_attn",
    )(q, k, v)


def _post_mlp_kernel(x_ref, ya_ref, yb_ref, wout_ref, gc_ref, wxq_ref, mem_ref, gmem_ref, wxkv_ref,
                     wxo_ref, gm_ref, wup_ref, wdown_ref, gf_ref, o_ref, kv_ref, *, final_norm):
    D = x_ref.shape[1]

    @pl.when(pl.program_id(0) == 0)
    def _():
        mn = _rms(mem_ref[...], gmem_ref[...]).astype(BF16)
        kv_ref[...] = jnp.dot(mn, wxkv_ref[...], preferred_element_type=F32).astype(BF16)

    y = jnp.concatenate([ya_ref[...], yb_ref[...]], axis=1)
    h = x_ref[...] + jnp.dot(y, wout_ref[...], preferred_element_type=F32)

    hn = _rms(h, gc_ref[...]).astype(BF16)
    xd = D // X_HEADS
    q = jnp.dot(hn, wxq_ref[...], preferred_element_type=F32) * (math.log2(math.e) / math.sqrt(xd))
    q = q.astype(BF16)
    outs = []
    for hd in range(X_HEADS):
        qh = q[:, hd * xd:(hd + 1) * xd]
        kh = kv_ref[:, hd * xd:(hd + 1) * xd]
        vh = kv_ref[:, D + hd * xd:D + (hd + 1) * xd]
        s = lax.dot_general(qh, kh, (((1,), (1,)), ((), ())), preferred_element_type=F32)
        p = jnp.exp2(s - jnp.max(s, axis=-1, keepdims=True))
        l = jnp.sum(p, axis=-1, keepdims=True)
        outs.append((jnp.dot(p.astype(BF16), vh, preferred_element_type=F32) / l).astype(BF16))
    h = h + jnp.dot(jnp.concatenate(outs, axis=1), wxo_ref[...], preferred_element_type=F32)

    hn = _rms(h, gm_ref[...]).astype(BF16)
    acts = []
    for c in range(wup_ref.shape[1] // D):
        a = jnp.dot(hn, wup_ref[:, c * D:(c + 1) * D], preferred_element_type=F32)
        acts.append(jnp.square(jnp.maximum(a, 0.0)).astype(BF16))
    h = h + jnp.dot(jnp.concatenate(acts, axis=1), wdown_ref[...], preferred_element_type=F32)
    if final_norm:
        h = _rms(h, gf_ref[...])
    o_ref[...] = h


def _post_mlp(x, ya, yb, mem, p, gf, final_norm):
    S, D = x.shape
    ts = min(ROW_TILE, S)
    row = lambda i: (i, 0)
    consts = [p["w_out"], p["g_cross"], p["w_xq"], mem, p["g_mem"], p["w_xkv"], p["w_xo"],
              p["g_mlp"], p["w_up"], p["w_down"], gf]
    return pl.pallas_call(
        functools.partial(_post_mlp_kernel, final_norm=final_norm),
        grid=(S // ts,),
        in_specs=[pl.BlockSpec((ts, D), row),
                  pl.BlockSpec((ts, ya.shape[1]), row),
                  pl.BlockSpec((ts, yb.shape[1]), row)] + [_const_spec(c.shape) for c in consts],
        out_specs=pl.BlockSpec((ts, D), row),
        out_shape=jax.ShapeDtypeStruct((S, D), F32),
        scratch_shapes=[pltpu.VMEM((mem.shape[0], p["w_xkv"].shape[1]), BF16)],
        compiler_params=pltpu.CompilerParams(
            dimension_semantics=("arbitrary",), vmem_limit_bytes=VMEM_LIMIT),
        name="post_mlp",
    )(x, ya, yb, *consts)


def _rope_inv_freq():
    inv = np.float32(ROPE_THETA) ** (-np.arange(0, ROT_DIM, 2, dtype=np.float32) / np.float32(ROT_DIM))
    return jnp.asarray(inv.reshape(ROT_DIM // 2, 1))


def _layer(h, pos_row, invf, mem, p, lam_init, gf, final_norm):
    S, D = h.shape
    td = min(DIFF_TILE, S)
    assert S % td == 0 and S % min(SB_TILE * SB_BLOCKS_PER_STEP, S) == 0 and S % min(ROW_TILE, S) == 0, S
    qt, ka, vtblk, qs, ks, vs, kn = _in_proj(h, pos_row, invf, p["g_mix"], p["w_in"], td)
    ya = _diff_attn(p["lam_vecs"], p["g_subln"], kn, qt, ka, vtblk, lam_init)
    yb = _sb_attn(qs, ks, vs, min(SB_TILE, S))
    return _post_mlp(h, ya, yb, mem, p, gf, final_norm)


def kernel(x, mem, positions, g_mix, w_in, lambda_q1, lambda_k1, lambda_q2, lambda_k2, g_subln, w_out, g_cross, g_mem, w_xq, w_xkv, w_xo, g_mlp, w_up, w_down, g_final):
    B, S, D = x.shape
    depth = w_in.shape[0]
    invf = _rope_inv_freq()
    gf = g_final.reshape(1, D).astype(F32)
    outs = []
    for b in range(B):
        h = x[b]
        pos_row = positions[b].reshape(1, S)
        for l in range(depth):
            lam_init = 0.8 - 0.6 * math.exp(-0.3 * l)
            p = {
                "g_mix": g_mix[l].reshape(1, D),
                "w_in": w_in[l],
                "lam_vecs": jnp.stack([lambda_q1[l], lambda_k1[l], lambda_q2[l], lambda_k2[l]]).astype(F32),
                "g_subln": g_subln[l].reshape(DA_V, 1).astype(F32),
                "w_out": w_out[l].astype(BF16),
                "g_cross": g_cross[l].reshape(1, D),
                "g_mem": g_mem[l].reshape(1, D),
                "w_xq": w_xq[l].astype(BF16),
                "w_xkv": w_xkv[l].astype(BF16),
                "w_xo": w_xo[l].astype(BF16),
                "g_mlp": g_mlp[l].reshape(1, D),
                "w_up": w_up[l].astype(BF16),
                "w_down": w_down[l].astype(BF16),
            }
            h = _layer(h, pos_row, invf, mem[b], p, lam_init, gf, l == depth - 1)
        outs.append(h)
    return jnp.stack(outs)
```

```python
import functools
import math

import numpy as np
import jax
import jax.numpy as jnp
from jax import lax
from jax.experimental import pallas as pl
from jax.experimental.pallas import tpu as pltpu

F32 = jnp.float32
BF16 = jnp.bfloat16

HEAD_DIM = 64
DA_HEADS = 4
DA_V = 2 * HEAD_DIM
SB_HEADS = 8
GROUP_COLS = 512
ROT_DIM = HEAD_DIM // 4
ROPE_THETA = 500000.0
X_HEADS = 4
EPS = 1e-6
LANES = 128
NEG_BIG = -1e30

SB_ZERO_LOG2 = 151.0
SB_LINEAR_ABOVE = 64.0

LAGGED_MAX_SCORE = 45.0
KEY_NORM_SLACK = 1.02

MIB = 1024 * 1024
VMEM_LIMIT = {"in_proj": 40 * MIB, "diff_attn": 56 * MIB, "sb_attn": 50 * MIB, "post_mlp": 52 * MIB}

ROW_TILE = 512
DIFF_TILE = 512
DIFF_HEADS_PER_STEP = 4
SB_TILE = 256
SB_BLOCKS_PER_STEP = 2
V_ROWS = DA_V + 16


def _rms(x, g):
    ms = jnp.mean(x * x, axis=-1, keepdims=True)
    return x * lax.rsqrt(ms + EPS) * g


def _const_spec(shape):
    nd = len(shape)
    return pl.BlockSpec(shape, lambda *_: (0,) * nd, pipeline_mode=pl.Buffered(1))


def _in_proj_kernel(x_ref, pos_ref, invf_ref, g_ref, w_ref,
                    qt_ref, ka_ref, vt_ref, qs_ref, ks_ref, vs_ref, kn_ref, wb_ref):
    ts = x_ref.shape[0]

    @pl.when(pl.program_id(0) == 0)
    def _():
        for idx in range(w_ref.shape[1] // GROUP_COLS):
            cols = slice(idx * GROUP_COLS, (idx + 1) * GROUP_COLS)
            wb_ref[:, cols] = w_ref[:, cols].astype(BF16)

    xn = _rms(x_ref[...], g_ref[...]).astype(BF16)

    half = ROT_DIM // 2
    ang = invf_ref[...] * pos_ref[...].astype(F32)
    cos8 = jnp.cos(ang)
    sin8 = jnp.sin(ang)
    one8 = jnp.ones_like(cos8)
    zero8 = jnp.zeros_like(cos8)
    groups = range(LANES // half)
    in_x1 = [g % (HEAD_DIM // half) == 0 for g in groups]
    in_x2 = [g % (HEAD_DIM // half) == 1 for g in groups]
    cos = jnp.concatenate([cos8 if a or b else one8 for a, b in zip(in_x1, in_x2)], axis=0).T
    s_up = jnp.concatenate([-sin8 if a else zero8 for a in in_x1], axis=0).T
    s_dn = jnp.concatenate([sin8 if b else zero8 for b in in_x2], axis=0).T

    def group(idx):
        w = wb_ref[:, idx * GROUP_COLS:(idx + 1) * GROUP_COLS]
        return jnp.dot(xn, w, preferred_element_type=F32)

    def rope(u, scale):
        parts = []
        for c in range(GROUP_COLS // LANES):
            uc = u[:, c * LANES:(c + 1) * LANES]
            r = (uc * cos + pltpu.roll(uc, LANES - ROT_DIM // 2, axis=1) * s_up
                 + pltpu.roll(uc, ROT_DIM // 2, axis=1) * s_dn)
            parts.append(r * scale)
        return jnp.concatenate(parts, axis=1)

    q_scale = math.log2(math.e) / math.sqrt(HEAD_DIM)

    qt = rope(group(0), q_scale).T
    zero_half = jnp.zeros((HEAD_DIM, ts), BF16)
    for hc in range(2 * DA_HEADS):
        c = hc % 2
        piece = qt[hc * HEAD_DIM:(hc + 1) * HEAD_DIM].astype(BF16)
        qt_ref[hc * LANES + c * HEAD_DIM:hc * LANES + (c + 1) * HEAD_DIM, :] = piece
        qt_ref[hc * LANES + (1 - c) * HEAD_DIM:hc * LANES + (2 - c) * HEAD_DIM, :] = zero_half

    ka = rope(group(1), 1.0)
    ka_ref[...] = ka.astype(BF16)
    for h in range(DA_HEADS):
        kh = ka[:, h * LANES:(h + 1) * LANES]
        norm2 = jnp.max(jnp.sum(kh * kh, axis=1, keepdims=True), axis=0, keepdims=True)
        kn_ref[0, :, h * LANES:(h + 1) * LANES] = jnp.broadcast_to(norm2, (1, LANES))

    vt = group(2).astype(BF16).T
    pad_row = lax.broadcasted_iota(jnp.int32, (V_ROWS - DA_V, ts), 0)
    ones_pad = jnp.where(pad_row == 0, 1.0, 0.0).astype(BF16)
    for h in range(DA_HEADS):
        vt_ref[0, h * V_ROWS:h * V_ROWS + DA_V, :] = vt[h * DA_V:(h + 1) * DA_V]
        vt_ref[0, h * V_ROWS + DA_V:(h + 1) * V_ROWS, :] = ones_pad

    qs_ref[...] = (group(3) * q_scale).astype(BF16)
    ks_ref[...] = group(4).astype(BF16)
    vs_ref[...] = group(5).astype(BF16)


def _in_proj(x, pos_row, invf, g, w, ts):
    S, D = x.shape
    nt = S // ts
    flat = jax.ShapeDtypeStruct((S, GROUP_COLS), BF16)
    row = lambda i: (i, 0)
    flat_spec = pl.BlockSpec((ts, GROUP_COLS), row)
    return pl.pallas_call(
        _in_proj_kernel,
        grid=(nt,),
        in_specs=[
            pl.BlockSpec((ts, D), row),
            pl.BlockSpec((1, ts), lambda i: (0, i)),
            _const_spec(invf.shape),
            _const_spec((1, D)),
            _const_spec(w.shape),
        ],
        out_specs=[
            pl.BlockSpec((2 * DA_HEADS * LANES, ts), lambda i: (0, i)),
            flat_spec,
            pl.BlockSpec((1, DA_HEADS * V_ROWS, ts), lambda i: (i, 0, 0)),
            flat_spec, flat_spec, flat_spec,
            pl.BlockSpec((1, 1, DA_HEADS * LANES), lambda i: (i, 0, 0)),
        ],
        out_shape=[
            jax.ShapeDtypeStruct((2 * DA_HEADS * LANES, S), BF16),
            flat,
            jax.ShapeDtypeStruct((nt, DA_HEADS * V_ROWS, ts), BF16),
            flat, flat, flat,
            jax.ShapeDtypeStruct((nt, 1, DA_HEADS * LANES), F32),
        ],
        scratch_shapes=[pltpu.VMEM(w.shape, BF16)],
        compiler_params=pltpu.CompilerParams(
            dimension_semantics=("arbitrary",), vmem_limit_bytes=VMEM_LIMIT["in_proj"]),
        name="in_proj",
    )(x, pos_row, invf, g, w)


def _diff_attn_kernel(lam_ref, g_ref, kn_ref, bias_ref, qt_ref, k_ref, vt_ref, o_ref,
                      acc_ref, m_ref, pa_ref, pb_ref, ma_ref, mb_ref, fa_ref, fb_ref, *, lam_init):
    i = pl.program_id(1)
    t = qt_ref.shape[1]
    heads = qt_ref.shape[0] // (2 * LANES)
    maps = [(h, 2 * h + c) for h in range(heads) for c in range(2)]
    acc_ref[...] = jnp.zeros_like(acc_ref)

    def scores(j, h, mp, diagonal):
        kb = k_ref[pl.ds(pl.multiple_of(j * t, t), t), h * LANES:(h + 1) * LANES]
        st = jnp.dot(kb, qt_ref[mp * LANES:(mp + 1) * LANES, :], preferred_element_type=F32)
        if diagonal:
            st = st + bias_ref[...]
        return st

    def values(j, h):
        return vt_ref[j, h * V_ROWS:(h + 1) * V_ROWS, :]

    def exact_block(j, diagonal):
        for h, mp in maps:
            st = scores(j, h, mp, diagonal)
            m_old = m_ref[mp]
            m_new = jnp.maximum(m_old, jnp.max(st, axis=0, keepdims=True))
            alpha = jnp.exp2(m_old - m_new)
            p = jnp.exp2(st - m_new).astype(BF16)
            acc_ref[mp] = alpha * acc_ref[mp] + jnp.dot(values(j, h), p,
                                                        preferred_element_type=F32)
            m_ref[mp] = m_new

    def lagged_weights(j, p_ref, m_in_ref, m_out_ref, f_ref, diagonal=False):
        for h, mp in maps:
            st = scores(j, h, mp, diagonal)
            m_old = m_in_ref[mp]
            p_ref[mp] = jnp.exp2(st - m_old).astype(BF16)
            m_new = jnp.maximum(m_old, jnp.max(st, axis=0, keepdims=True))
            f_ref[mp] = jnp.exp2(m_old - m_new)
            m_out_ref[mp] = m_new

    def lagged_accumulate(j, p_ref, f_ref):
        for h, mp in maps:
            acc_ref[mp] = f_ref[mp] * (acc_ref[mp] + jnp.dot(values(j, h), p_ref[mp],
                                                            preferred_element_type=F32))

    tile = lax.broadcasted_iota(jnp.int32, kn_ref.shape, 0)
    kmax2 = jnp.max(jnp.where(tile <= i, kn_ref[...], 0.0)) * KEY_NORM_SLACK
    qn2 = jnp.zeros((1, t), F32)
    for _, mp in maps:
        qm = qt_ref[mp * LANES:(mp + 1) * LANES, :].astype(F32)
        qn2 = jnp.maximum(qn2, jnp.sum(qm * qm, axis=0, keepdims=True))
    safe = jnp.max(qn2) * kmax2 <= LAGGED_MAX_SCORE * LAGGED_MAX_SCORE

    @pl.when(safe)
    def _():
        mb_ref[...] = jnp.zeros_like(mb_ref)

        @pl.when(i == 0)
        def _():
            lagged_weights(0, pa_ref, mb_ref, ma_ref, fa_ref, diagonal=True)
            lagged_accumulate(0, pa_ref, fa_ref)

        @pl.when(i > 0)
        def _():
            lagged_weights(0, pa_ref, mb_ref, ma_ref, fa_ref)

            def pair(n, carry):
                j = 2 * n
                lagged_weights(j + 1, pb_ref, ma_ref, mb_ref, fb_ref)
                lagged_accumulate(j, pa_ref, fa_ref)
                lagged_weights(j + 2, pa_ref, mb_ref, ma_ref, fa_ref)
                lagged_accumulate(j + 1, pb_ref, fb_ref)
                return carry

            lax.fori_loop(0, (i - 1) // 2, pair, 0)

            @pl.when(i % 2 == 1)
            def _():
                lagged_weights(i, pb_ref, ma_ref, mb_ref, fb_ref, diagonal=True)
                lagged_accumulate(i - 1, pa_ref, fa_ref)
                lagged_accumulate(i, pb_ref, fb_ref)

            @pl.when(i % 2 == 0)
            def _():
                lagged_weights(i - 1, pb_ref, ma_ref, mb_ref, fb_ref)
                lagged_accumulate(i - 2, pa_ref, fa_ref)
                lagged_weights(i, pa_ref, mb_ref, ma_ref, fa_ref, diagonal=True)
                lagged_accumulate(i - 1, pb_ref, fb_ref)
                lagged_accumulate(i, pa_ref, fa_ref)

    @pl.when(jnp.logical_not(safe))
    def _():
        m_ref[...] = jnp.full_like(m_ref, NEG_BIG)

        def one(j, carry):
            exact_block(j, False)
            return carry

        lax.fori_loop(0, i, one, 0)
        exact_block(i, True)

    lam_v = lam_ref[...]
    lam = (jnp.exp(jnp.sum(lam_v[0:1] * lam_v[1:2], axis=1, keepdims=True))
           - jnp.exp(jnp.sum(lam_v[2:3] * lam_v[3:4], axis=1, keepdims=True)) + lam_init)
    for h in range(heads):
        a0 = acc_ref[2 * h]
        a1 = acc_ref[2 * h + 1]
        o = a0[:DA_V] / a0[DA_V:DA_V + 1] - lam * (a1[:DA_V] / a1[DA_V:DA_V + 1])
        ms = jnp.mean(o * o, axis=0, keepdims=True)
        y = o * lax.rsqrt(ms + EPS) * g_ref[...] * (1.0 - lam_init)
        o_ref[:, h * DA_V:(h + 1) * DA_V] = y.T.astype(o_ref.dtype)


def _diff_attn(lam_vecs, g_col, kn, qt, ka, vtblk, lam_init):
    S = ka.shape[0]
    nk, _, t = vtblk.shape
    key, qry = np.arange(t)[:, None], np.arange(t)[None, :]
    causal_bias = jnp.asarray(np.where(key <= qry, 0.0, NEG_BIG), F32)
    hg = DIFF_HEADS_PER_STEP
    nm = 2 * hg
    col_block = lambda shape: pl.BlockSpec(shape, lambda g, i: (0,) * (len(shape) - 1) + (g,),
                                           pipeline_mode=pl.Buffered(1))
    return pl.pallas_call(
        functools.partial(_diff_attn_kernel, lam_init=lam_init),
        grid=(DA_HEADS // hg, nk),
        in_specs=[
            _const_spec(lam_vecs.shape),
            _const_spec(g_col.shape),
            col_block((nk, 1, hg * LANES)),
            _const_spec((t, t)),
            pl.BlockSpec((nm * LANES, t), lambda g, i: (g, i)),
            col_block((S, hg * LANES)),
            pl.BlockSpec((nk, hg * V_ROWS, t), lambda g, i: (0, g, 0),
                         pipeline_mode=pl.Buffered(1)),
        ],
        out_specs=pl.BlockSpec((t, hg * DA_V), lambda g, i: (i, g)),
        out_shape=jax.ShapeDtypeStruct((S, DA_HEADS * DA_V), BF16),
        scratch_shapes=[pltpu.VMEM((nm, V_ROWS, t), F32), pltpu.VMEM((nm, 1, t), F32),
                        pltpu.VMEM((nm, t, t), BF16), pltpu.VMEM((nm, t, t), BF16),
                        pltpu.VMEM((nm, 1, t), F32), pltpu.VMEM((nm, 1, t), F32),
                        pltpu.VMEM((nm, 1, t), F32), pltpu.VMEM((nm, 1, t), F32)],
        compiler_params=pltpu.CompilerParams(
            dimension_semantics=("arbitrary", "arbitrary"), vmem_limit_bytes=VMEM_LIMIT["diff_attn"]),
        name="diff_attn",
    )(lam_vecs, g_col, kn, causal_bias, qt, ka, vtblk)


def _sb_attn_kernel(q_ref, knew_ref, vnew_ref, o_ref, k_ref, v_ref, acc_ref, carry_ref, *, t):
    i = pl.program_id(0)
    nb = q_ref.shape[0] // t
    pairs = q_ref.shape[1] // LANES
    own = pl.ds(pl.multiple_of(i * nb * t, nb * t), nb * t)
    k_ref[own, :] = knew_ref[...]
    v_ref[own, :] = vnew_ref[...]
    lane = lax.broadcasted_iota(jnp.int32, (t, LANES), 1)
    first = lane < HEAD_DIM
    row = lax.broadcasted_iota(jnp.int32, (t, t), 0)
    col = lax.broadcasted_iota(jnp.int32, (t, t), 1)
    tri = jnp.where(row > col, 1.0, 0.0).astype(BF16)
    strict = jnp.concatenate([col < row, col < row], axis=0)

    def block(j, b, p, diagonal):
        lanes = slice(p * LANES, (p + 1) * LANES)
        slot = b * pairs + p
        q2 = q_ref[b * t:(b + 1) * t, lanes]
        zeros = jnp.zeros_like(q2)
        qs = jnp.concatenate([jnp.where(first, q2, zeros), jnp.where(first, zeros, q2)], axis=0)
        rows = pl.ds(pl.multiple_of(j * t, t), t)
        kb = k_ref[rows, lanes]
        vb = v_ref[rows, lanes]
        z = lax.dot_general(qs, kb, (((1,), (1,)), ((), ())),
                            preferred_element_type=F32)
        if diagonal:
            z = jnp.where(strict, z, NEG_BIG)
            carry = jnp.zeros((2 * t, 1), F32)
        else:
            carry = carry_ref[slot]
        sp = jnp.where(z > SB_LINEAR_ABOVE, z, jnp.log2(1.0 + jnp.exp2(z)))
        hi = sp.astype(BF16)
        between = jnp.dot(hi, tri, preferred_element_type=F32) + carry
        a = jnp.exp2(z - sp - between)
        av = jnp.dot(a.astype(BF16), vb, preferred_element_type=F32)
        acc_ref[slot] = av if diagonal else acc_ref[slot] + av
        carry = carry + jnp.sum(sp, axis=1, keepdims=True)
        carry_ref[slot] = carry
        return carry

    def sweep(back, blocks, diagonal):
        low = None
        for b in blocks:
            for p in range(pairs):
                carry = block(i * nb + b - back, b, p, diagonal)
                low = carry if low is None else jnp.minimum(low, carry)
        return jnp.min(low)

    everyone = range(nb)

    def cond(state):
        return jnp.logical_and(state[0] <= i * nb, state[1] < SB_ZERO_LOG2)

    def body(state):
        return state[0] + 1, sweep(state[0], everyone, False)

    back, _ = lax.while_loop(cond, body, (1, sweep(0, everyone, True)))

    for extra in range(1, nb):
        later = range(extra, nb)
        pending = carry_ref[later[0] * pairs]
        for slot in range(later[0] * pairs + 1, nb * pairs):
            pending = jnp.minimum(pending, carry_ref[slot])

        @pl.when(jnp.logical_and(back == i * nb + extra, jnp.min(pending) < SB_ZERO_LOG2))
        def _():
            sweep(i * nb + extra, later, False)

        back = jnp.where(back == i * nb + extra, back + 1, back)

    for b in range(nb):
        for p in range(pairs):
            acc = acc_ref[b * pairs + p]
            o_ref[b * t:(b + 1) * t, p * LANES:(p + 1) * LANES] = jnp.where(
                first, acc[:t], acc[t:]).astype(o_ref.dtype)


def _sb_attn(q, k, v, t):
    S, cols = q.shape
    pairs = cols // LANES
    nb = min(SB_BLOCKS_PER_STEP, S // t)
    step = pl.BlockSpec((nb * t, cols), lambda i: (i, 0))
    return pl.pallas_call(
        functools.partial(_sb_attn_kernel, t=t),
        grid=(S // (nb * t),),
        in_specs=[step, step, step],
        out_specs=step,
        out_shape=jax.ShapeDtypeStruct((S, cols), BF16),
        scratch_shapes=[pltpu.VMEM((S, cols), BF16), pltpu.VMEM((S, cols), BF16),
                        pltpu.VMEM((nb * pairs, 2 * t, LANES), F32),
                        pltpu.VMEM((nb * pairs, 2 * t, 1), F32)],
        compiler_params=pltpu.CompilerParams(
            dimension_semantics=("arbitrary",), vmem_limit_bytes=VMEM_LIMIT["sb_attn"]),
        name="sb_attn",
    )(q, k, v)


def _post_mlp_kernel(x_ref, ya_ref, yb_ref, wout_ref, gc_ref, wxq_ref, mem_ref, gmem_ref, wxkv_ref,
                     wxo_ref, gm_ref, wup_ref, wdown_ref, gf_ref, o_ref, kv_ref, *, final_norm):
    D = x_ref.shape[1]

    @pl.when(pl.program_id(0) == 0)
    def _():
        mn = _rms(mem_ref[...], gmem_ref[...]).astype(BF16)
        kv_ref[...] = jnp.dot(mn, wxkv_ref[...], preferred_element_type=F32).astype(BF16)

    y = jnp.concatenate([ya_ref[...], yb_ref[...]], axis=1)
    h = x_ref[...] + jnp.dot(y, wout_ref[...], preferred_element_type=F32)

    hn = _rms(h, gc_ref[...]).astype(BF16)
    xd = D // X_HEADS
    q = jnp.dot(hn, wxq_ref[...], preferred_element_type=F32) * (math.log2(math.e) / math.sqrt(xd))
    q = q.astype(BF16)
    outs = []
    for hd in range(X_HEADS):
        qh = q[:, hd * xd:(hd + 1) * xd]
        kh = kv_ref[:, hd * xd:(hd + 1) * xd]
        vh = kv_ref[:, D + hd * xd:D + (hd + 1) * xd]
        s = lax.dot_general(qh, kh, (((1,), (1,)), ((), ())), preferred_element_type=F32)
        p = jnp.exp2(s - jnp.max(s, axis=-1, keepdims=True))
        l = jnp.sum(p, axis=-1, keepdims=True)
        outs.append((jnp.dot(p.astype(BF16), vh, preferred_element_type=F32) / l).astype(BF16))
    h = h + jnp.dot(jnp.concatenate(outs, axis=1), wxo_ref[...], preferred_element_type=F32)

    hn = _rms(h, gm_ref[...]).astype(BF16)
    acts = []
    for c in range(wup_ref.shape[1] // D):
        a = jnp.dot(hn, wup_ref[:, c * D:(c + 1) * D], preferred_element_type=F32)
        acts.append(jnp.square(jnp.maximum(a, 0.0)).astype(BF16))
    h = h + jnp.dot(jnp.concatenate(acts, axis=1), wdown_ref[...], preferred_element_type=F32)
    if final_norm:
        h = _rms(h, gf_ref[...])
    o_ref[...] = h


def _post_mlp(x, ya, yb, mem, p, gf, final_norm):
    S, D = x.shape
    ts = min(ROW_TILE, S)
    row = lambda i: (i, 0)
    consts = [p["w_out"], p["g_cross"], p["w_xq"], mem, p["g_mem"], p["w_xkv"], p["w_xo"],
              p["g_mlp"], p["w_up"], p["w_down"], gf]
    return pl.pallas_call(
        functools.partial(_post_mlp_kernel, final_norm=final_norm),
        grid=(S // ts,),
        in_specs=[pl.BlockSpec((ts, D), row),
                  pl.BlockSpec((ts, ya.shape[1]), row),
                  pl.BlockSpec((ts, yb.shape[1]), row)] + [_const_spec(c.shape) for c in consts],
        out_specs=pl.BlockSpec((ts, D), row),
        out_shape=jax.ShapeDtypeStruct((S, D), F32),
        scratch_shapes=[pltpu.VMEM((mem.shape[0], p["w_xkv"].shape[1]), BF16)],
        compiler_params=pltpu.CompilerParams(
            dimension_semantics=("arbitrary",), vmem_limit_bytes=VMEM_LIMIT["post_mlp"]),
        name="post_mlp",
    )(x, ya, yb, *consts)


def _rope_inv_freq():
    inv = np.float32(ROPE_THETA) ** (-np.arange(0, ROT_DIM, 2, dtype=np.float32) / np.float32(ROT_DIM))
    return jnp.asarray(inv.reshape(ROT_DIM // 2, 1))


def _layer(h, pos_row, invf, mem, p, lam_init, gf, final_norm):
    S, D = h.shape
    td = min(DIFF_TILE, S)
    assert S % td == 0 and S % min(SB_TILE * SB_BLOCKS_PER_STEP, S) == 0 and S % min(ROW_TILE, S) == 0, S
    qt, ka, vtblk, qs, ks, vs, kn = _in_proj(h, pos_row, invf, p["g_mix"], p["w_in"], td)
    ya = _diff_attn(p["lam_vecs"], p["g_subln"], kn, qt, ka, vtblk, lam_init)
    yb = _sb_attn(qs, ks, vs, min(SB_TILE, S))
    return _post_mlp(h, ya, yb, mem, p, gf, final_norm)


def kernel(x, mem, positions, g_mix, w_in, lambda_q1, lambda_k1, lambda_q2, lambda_k2, g_subln, w_out, g_cross, g_mem, w_xq, w_xkv, w_xo, g_mlp, w_up, w_down, g_final):
    B, S, D = x.shape
    depth = w_in.shape[0]
    invf = _rope_inv_freq()
    gf = g_final.reshape(1, D).astype(F32)
    outs = []
    for b in range(B):
        h = x[b]
        pos_row = positions[b].reshape(1, S)
        for l in range(depth):
            lam_init = 0.8 - 0.6 * math.exp(-0.3 * l)
            p = {
                "g_mix": g_mix[l].reshape(1, D),
                "w_in": w_in[l],
                "lam_vecs": jnp.stack([lambda_q1[l], lambda_k1[l], lambda_q2[l], lambda_k2[l]]).astype(F32),
                "g_subln": g_subln[l].reshape(DA_V, 1).astype(F32),
                "w_out": w_out[l].astype(BF16),
                "g_cross": g_cross[l].reshape(1, D),
                "g_mem": g_mem[l].reshape(1, D),
                "w_xq": w_xq[l].astype(BF16),
                "w_xkv": w_xkv[l].astype(BF16),
                "w_xo": w_xo[l].astype(BF16),
                "g_mlp": g_mlp[l].reshape(1, D),
                "w_up": w_up[l].astype(BF16),
                "w_down": w_down[l].astype(BF16),
            }
            h = _layer(h, pos_row, invf, mem[b], p, lam_init, gf, l == depth - 1)
        outs.append(h)
    return jnp.stack(outs)
```
